```python
import jax, jax.numpy as jnp
from jax import lax
import numpy as np

D_MODEL = 1024
BATCH = 8
SEQ = 2048
DEPTH = 4
DEC_BATCH = 128
DEC_SEQ = 8
PAST_LEN = 2048
PAGE_SIZE = 128

N_MIXERS = 2
N_A_LAYERS = (DEPTH + 1) // 2
N_B_LAYERS = DEPTH // 2
P_DIM = 256
EPS = 1e-6
ROPE_THETA = 10000.0

CHUNK = 128
A_WIDTH = 2 * D_MODEL
A_GROUPS = 8
A_GROUP_DIM = A_WIDTH // A_GROUPS

N_HEADS = 8
HEAD_DIM = D_MODEL // N_HEADS
N_KV_HEADS = 2
B_WIDTH = N_HEADS * HEAD_DIM
KV_WIDTH = N_KV_HEADS * HEAD_DIM
IDX_HEADS = 8
IDX_DIM = 64
TOPK_MAX = 256
Q_BLOCK = 128
B_SPLITS = (B_WIDTH, KV_WIDTH, KV_WIDTH, IDX_HEADS * IDX_DIM, IDX_DIM, IDX_HEADS, B_WIDTH)
B_IN = sum(B_SPLITS)

kernel_name = 'hybrid_gmlp_dsa_decode_step'


def rmsnorm(x, g):
    xf = x.astype(jnp.float32)
    r = lax.rsqrt(jnp.mean(xf * xf, axis=-1, keepdims=True) + EPS)
    return (xf * r).astype(x.dtype) * g


def rope(x, pos):
    half = x.shape[-1] // 2
    inv = ROPE_THETA ** (-jnp.arange(half, dtype=jnp.float32) / half)
    ang = pos.astype(jnp.float32)[:, None] * inv[None, :]
    cos = jnp.cos(ang)[:, None, :]
    sin = jnp.sin(ang)[:, None, :]
    x1, x2 = x[..., :half], x[..., half:]
    return jnp.concatenate([x1 * cos - x2 * sin, x2 * cos + x1 * sin], axis=-1).astype(x.dtype)


def mixer_a(h, w_in, w_s, b_s, g_v, w_out):
    bsz, t, _ = h.shape
    u, v, z = jnp.split(h @ w_in, 3, axis=-1)
    v = rmsnorm(v, g_v)
    c = min(t, CHUNK)
    n_chunks = t // c
    causal = jnp.tril(jnp.ones((c, c), dtype=bool))
    ws = jnp.where(causal[None], w_s[:, :c, :c], jnp.zeros((), w_s.dtype))
    vc = v.reshape(bsz, n_chunks, c, A_GROUPS, A_GROUP_DIM)
    mixed = jnp.einsum('gts,bnsgd->bntgd', ws, vc) + b_s[:, :c].T[None, None, :, :, None]
    y = u * mixed.reshape(bsz, t, A_WIDTH) * jax.nn.silu(z)
    return y @ w_out, v


def project_b(h, pos, w_in, g_q, g_k, g_ik):
    bsz, t, _ = h.shape
    pts = np.cumsum(B_SPLITS)[:-1].tolist()
    q, k, v, iq, ik, iw, z = jnp.split(h @ w_in, pts, axis=-1)
    q = rope(rmsnorm(q.reshape(bsz, t, N_HEADS, HEAD_DIM), g_q), pos)
    k = rope(rmsnorm(k.reshape(bsz, t, N_KV_HEADS, HEAD_DIM), g_k), pos)
    v = v.reshape(bsz, t, N_KV_HEADS, HEAD_DIM)
    iq = rope(iq.reshape(bsz, t, IDX_HEADS, IDX_DIM), pos)
    ik = rope(rmsnorm(ik, g_ik)[:, :, None, :], pos)[:, :, 0, :]
    return q, k, v, iq, ik, iw, z


def indexer_scores(iq, iw, ik):
    s = jax.nn.relu(jnp.einsum('bthd,bsd->bths', iq, ik).astype(jnp.float32))
    return jnp.einsum('bths,bth->bts', s, iw.astype(jnp.float32)) * (IDX_DIM ** -0.5 * IDX_HEADS ** -0.5)


def attend_selected(q, k_sel, v_sel, valid):
    bsz, t, h, d = q.shape
    qg = q.reshape(bsz, t, N_KV_HEADS, h // N_KV_HEADS, d)
    s = jnp.einsum('btgrd,btkgd->btgrk', qg, k_sel).astype(jnp.float32) * (d ** -0.5)
    s = jnp.where(valid[:, :, None, None, :], s, -jnp.inf)
    p = jax.nn.softmax(s, axis=-1).astype(v_sel.dtype)
    o = jnp.einsum('btgrk,btkgd->btgrd', p, v_sel)
    return o.reshape(bsz, t, h * d)


def dsa_prompt(q, k, v, iq, ik, iw):
    bsz, s = q.shape[:2]
    topk = min(TOPK_MAX, s // 4)
    n_blocks = s // Q_BLOCK
    key_pos = jnp.arange(s)
    bi = jnp.arange(bsz)[:, None, None]

    def block(i):
        start = i * Q_BLOCK
        qb = lax.dynamic_slice_in_dim(q, start, Q_BLOCK, axis=1)
        iqb = lax.dynamic_slice_in_dim(iq, start, Q_BLOCK, axis=1)
        iwb = lax.dynamic_slice_in_dim(iw, start, Q_BLOCK, axis=1)
        qpos = start + jnp.arange(Q_BLOCK)
        sc = indexer_scores(iqb, iwb, ik)
        adm = key_pos[None, :] <= qpos[:, None]
        sc = jnp.where(adm[None], sc, -jnp.inf)
        _, idx = lax.top_k(sc, topk)
        valid = idx <= qpos[None, :, None]
        return attend_selected(qb, k[bi, idx], v[bi, idx], valid)

    out = lax.map(block, jnp.arange(n_blocks))
    return out.transpose(1, 0, 2, 3).reshape(bsz, s, B_WIDTH)


def dsa_sample(q, k_new, v_new, iq, ik_new, iw, cache_k, cache_v, cache_ik, page_table, layer):
    db, t = q.shape[:2]
    n_pages = page_table.shape[1]
    past = n_pages * PAGE_SIZE
    L = past + t
    topk = min(TOPK_MAX, L // 4)
    ik_past = cache_ik[page_table, layer].reshape(db, past, IDX_DIM)
    ik_all = jnp.concatenate([ik_past, ik_new], axis=1)
    sc = indexer_scores(iq, iw, ik_all)
    qpos = past + jnp.arange(t)
    adm = jnp.arange(L)[None, :] <= qpos[:, None]
    sc = jnp.where(adm[None], sc, -jnp.inf)
    _, idx = lax.top_k(sc, topk)
    valid = idx <= qpos[None, :, None]
    from_past = idx < past
    pidx = jnp.minimum(idx, past - 1)
    phys = jnp.take_along_axis(page_table, (pidx // PAGE_SIZE).reshape(db, t * topk), axis=1).reshape(db, t, topk)
    slot = pidx % PAGE_SIZE
    nidx = jnp.clip(idx - past, 0, t - 1)
    bi = jnp.arange(db)[:, None, None]
    sel = from_past[..., None, None]
    k_sel = jnp.where(sel, cache_k[phys, layer, slot], k_new[bi, nidx])
    v_sel = jnp.where(sel, cache_v[phys, layer, slot], v_new[bi, nidx])
    return attend_selected(q, k_sel, v_sel, valid)


def per_layer_embed(x, p, g, w_gate, w_pe):
    return x + jax.nn.sigmoid(rmsnorm(x, g) @ w_gate) * (p @ w_pe)


def setup_inputs(seed: int = 0) -> dict:
    key = jax.random.key(seed)
    ks = jax.random.split(key, 24)
    n_pages = PAST_LEN // PAGE_SIZE
    n_pool = (DEC_BATCH * n_pages * 5) // 4
    nrm = lambda k, shape, scale=1.0: jax.random.normal(k, shape, jnp.float32) * scale
    page_table = jax.random.permutation(ks[5], n_pool)[:DEC_BATCH * n_pages].reshape(DEC_BATCH, n_pages).astype(jnp.int32)
    return {
        'x_prompt': nrm(ks[0], (BATCH, SEQ, D_MODEL)),
        'x_sample': nrm(ks[1], (DEC_BATCH, DEC_SEQ, D_MODEL)),
        'cache_k': nrm(ks[2], (n_pool, N_B_LAYERS, PAGE_SIZE, N_KV_HEADS, HEAD_DIM)),
        'cache_v': nrm(ks[3], (n_pool, N_B_LAYERS, PAGE_SIZE, N_KV_HEADS, HEAD_DIM)),
        'cache_ik': nrm(ks[4], (n_pool, N_B_LAYERS, PAGE_SIZE, IDX_DIM)),
        'page_table': page_table,
        'p_prompt': nrm(ks[6], (DEPTH, BATCH, SEQ, P_DIM)),
        'p_sample': nrm(ks[7], (DEPTH, DEC_BATCH, DEC_SEQ, P_DIM)),
        'g_norm': 1.0 + nrm(ks[8], (DEPTH, D_MODEL), 0.05),
        'w_in_a': nrm(ks[9], (N_A_LAYERS, D_MODEL, 3 * A_WIDTH), D_MODEL ** -0.5),
        'w_s_a': nrm(ks[10], (N_A_LAYERS, A_GROUPS, CHUNK, CHUNK), CHUNK ** -0.5),
        'b_s_a': 1.0 + nrm(ks[11], (N_A_LAYERS, A_GROUPS, CHUNK), 0.1),
        'g_v_a': 1.0 + nrm(ks[12], (N_A_LAYERS, A_WIDTH), 0.05),
        'w_out_a': nrm(ks[13], (N_A_LAYERS, A_WIDTH, D_MODEL), A_WIDTH ** -0.5),
        'w_in_b': nrm(ks[14], (N_B_LAYERS, D_MODEL, B_IN), D_MODEL ** -0.5),
        'g_q_b': 1.0 + nrm(ks[15], (N_B_LAYERS, HEAD_DIM), 0.05),
        'g_k_b': 1.0 + nrm(ks[16], (N_B_LAYERS, HEAD_DIM), 0.05),
        'g_ik_b': 1.0 + nrm(ks[17], (N_B_LAYERS, IDX_DIM), 0.05),
        'w_out_b': nrm(ks[18], (N_B_LAYERS, B_WIDTH, D_MODEL), B_WIDTH ** -0.5),
        'g_pe': 1.0 + nrm(ks[19], (DEPTH, D_MODEL), 0.05),
        'w_pe_gate': nrm(ks[20], (DEPTH, D_MODEL, D_MODEL), D_MODEL ** -0.5),
        'w_pe_proj': nrm(ks[21], (DEPTH, P_DIM, D_MODEL), P_DIM ** -0.5),
    }


def reference(x_prompt, x_sample, cache_k, cache_v, cache_ik, page_table, p_prompt, p_sample,
              g_norm, w_in_a, w_s_a, b_s_a, g_v_a, w_out_a, w_in_b, g_q_b, g_k_b, g_ik_b, w_out_b,
              g_pe, w_pe_gate, w_pe_proj):
    yp, ys = x_prompt, x_sample
    past = page_table.shape[1] * PAGE_SIZE
    pos_p = jnp.arange(x_prompt.shape[1])
    pos_s = past + jnp.arange(x_sample.shape[1])
    kp_l, vp_l, ikp_l, ks_l, vs_l, iks_l, cv_l = [], [], [], [], [], [], []
    for i in range(DEPTH):
        j = i // N_MIXERS
        hp = rmsnorm(yp, g_norm[i])
        hs = rmsnorm(ys, g_norm[i])
        if i % N_MIXERS == 0:
            op, _ = mixer_a(hp, w_in_a[j], w_s_a[j], b_s_a[j], g_v_a[j], w_out_a[j])
            os_, v_rows = mixer_a(hs, w_in_a[j], w_s_a[j], b_s_a[j], g_v_a[j], w_out_a[j])
            cv_l.append(v_rows)
        else:
            q, k, v, iq, ik, iw, z = project_b(hp, pos_p, w_in_b[j], g_q_b[j], g_k_b[j], g_ik_b[j])
            op = (dsa_prompt(q, k, v, iq, ik, iw) * jax.nn.silu(z)) @ w_out_b[j]
            kp_l.append(k)
            vp_l.append(v)
            ikp_l.append(ik)
            q, k, v, iq, ik, iw, z = project_b(hs, pos_s, w_in_b[j], g_q_b[j], g_k_b[j], g_ik_b[j])
            att = dsa_sample(q, k, v, iq, ik, iw, cache_k, cache_v, cache_ik, page_table, j)
            os_ = (att * jax.nn.silu(z)) @ w_out_b[j]
            ks_l.append(k)
            vs_l.append(v)
            iks_l.append(ik)
        yp = per_layer_embed(yp + op, p_prompt[i], g_pe[i], w_pe_gate[i], w_pe_proj[i])
        ys = per_layer_embed(ys + os_, p_sample[i], g_pe[i], w_pe_gate[i], w_pe_proj[i])
    k_prompt = jnp.stack(kp_l, axis=1)
    v_prompt = jnp.stack(vp_l, axis=1)
    ik_prompt = jnp.stack(ikp_l, axis=1)
    k_sample = jnp.stack(ks_l, axis=1)
    v_sample = jnp.stack(vs_l, axis=1)
    ik_sample = jnp.stack(iks_l, axis=1)
    chunk_v_sample = jnp.stack(cv_l, axis=1)
    return (yp, ys, k_prompt, v_prompt, ik_prompt, k_sample, v_sample, ik_sample, chunk_v_sample)
```

```python
import functools

import jax
import jax.numpy as jnp
from jax import lax
from jax.experimental import pallas as pl
from jax.experimental.pallas import tpu as pltpu

EPS = 1e-6
ROPE_THETA = 10000.0
CHUNK = 128
A_GROUPS = 8
N_HEADS = 8
N_KV_HEADS = 2
IDX_HEADS = 8
IDX_DIM = 64
TOPK_MAX = 256
PAGE_SIZE = 128

LANES = 128
INT_MIN = -(2 ** 31)
NEG = -1e30
VMEM_LIMIT = 56 * 1024 * 1024

F32 = jnp.float32
BF16 = jnp.bfloat16
I32 = jnp.int32


def _bf(x):
    return x.astype(BF16)


def _dot(a, b):
    return jnp.dot(a, b, preferred_element_type=F32)


def _dot_nt(a, b):
    return lax.dot_general(a, b, (((1,), (1,)), ((), ())), preferred_element_type=F32)


def _rms(x, g):
    r = lax.rsqrt(jnp.mean(x * x, axis=-1, keepdims=True) + EPS)
    return x * r * g


def _silu(z):
    return z * jax.nn.sigmoid(z)


def _per_layer_embed(x1, p, gpe, wg, wpe):
    gate = jax.nn.sigmoid(_dot(_bf(_rms(x1, gpe)), wg))
    return x1 + gate * _dot(_bf(p), wpe)


def _sort_key(s):
    bits = pltpu.bitcast(s, I32)
    mag = bits & jnp.int32(0x7FFFFFFF)
    return jnp.where(bits < 0, -mag, bits)


def _const_spec(shape):
    nd = len(shape)
    return pl.BlockSpec(shape, lambda *_: (0,) * nd, pipeline_mode=pl.Buffered(1))


def _params(sem):
    return pltpu.CompilerParams(dimension_semantics=sem, vmem_limit_bytes=VMEM_LIMIT)


def _layer_a_kernel(x_ref, p_ref, gn_ref, win_ref, ws_ref, bs_ref, gv_ref, wout_ref,
                    gpe_ref, wg_ref, wpe_ref, xo_ref, *v_out, chunk):
    aw = gv_ref.shape[1]
    gd = aw // A_GROUPS
    x = x_ref[...]
    tm = x.shape[0]
    h = _bf(_rms(x, gn_ref[...]))
    u = _dot(h, win_ref[:, 0:aw])
    v = _rms(_dot(h, win_ref[:, aw:2 * aw]), gv_ref[...])
    z = _dot(h, win_ref[:, 2 * aw:3 * aw])
    if v_out:
        v_out[0][...] = v
    vb = _bf(v)
    row = lax.broadcasted_iota(I32, (CHUNK, CHUNK), 0)
    col = lax.broadcasted_iota(I32, (CHUNK, CHUNK), 1)
    keep = col <= row
    if chunk != CHUNK:
        keep = jnp.logical_and(keep, (row // chunk) == (col // chunk))
    cols = []
    for g in range(A_GROUPS):
        wsg = _bf(jnp.where(keep, ws_ref[g], 0.0))
        rows = [_dot(wsg, vb[j * CHUNK:(j + 1) * CHUNK, g * gd:(g + 1) * gd]) for j in range(tm // CHUNK)]
        cols.append(rows[0] if len(rows) == 1 else jnp.concatenate(rows, axis=0))
    mixed = jnp.concatenate(cols, axis=1)
    bias = bs_ref[...]
    if tm != CHUNK:
        bias = jnp.concatenate([bias] * (tm // CHUNK), axis=0)
    y = u * (mixed + bias) * _silu(z)
    x1 = x + _dot(_bf(y), wout_ref[...])
    xo_ref[...] = _per_layer_embed(x1, p_ref[...], gpe_ref[...], wg_ref[...], wpe_ref[...])


def _layer_a(x, p, gn, win, ws_t, bs_t, gv, wout, gpe, wg, wpe, *, chunk, tm, want_v):
    n, d = x.shape
    aw = gv.shape[1]
    pd = p.shape[1]
    row_spec = lambda w: pl.BlockSpec((tm, w), lambda i: (i, 0))
    out_shape = [jax.ShapeDtypeStruct((n, d), F32)]
    out_specs = [row_spec(d)]
    if want_v:
        out_shape.append(jax.ShapeDtypeStruct((n, aw), F32))
        out_specs.append(row_spec(aw))
    res = pl.pallas_call(
        functools.partial(_layer_a_kernel, chunk=chunk),
        grid=(n // tm,),
        in_specs=[row_spec(d), row_spec(pd), _const_spec((1, d)), _const_spec((d, 3 * aw)),
                  _const_spec((A_GROUPS, CHUNK, CHUNK)), _const_spec((CHUNK, aw)), _const_spec((1, aw)),
                  _const_spec((aw, d)), _const_spec((1, d)), _const_spec((d, d)), _const_spec((pd, d))],
        out_specs=out_specs,
        out_shape=out_shape,
        compiler_params=_params(("arbitrary",)),
        name="layer_a",
    )(x, p, gn, win, ws_t, bs_t, gv, wout, gpe, wg, wpe)
    return res if want_v else (res[0], None)


def _proj_b_kernel(x_ref, gn_ref, wmain_ref, wsmall_ref, wiwt_ref, gq_ref, gk_ref, gik_ref,
                   c128_ref, s128_ref, c64_ref, s64a_ref, s64b_ref,
                   q_ref, k_ref, v_ref, ik_ref, kbf_ref, vt_ref, ikbf_ref, iq_ref, iw_ref, iwt_ref, sz_ref, *, kc):
    dh = LANES
    bw = N_HEADS * dh
    kvw = N_KV_HEADS * dh
    iqw = IDX_HEADS * IDX_DIM
    x = x_ref[...]
    tm = x.shape[0]
    h = _bf(_rms(x, gn_ref[...]))
    main = _dot(h, wmain_ref[...])
    small = _dot(h, wsmall_ref[...])
    iwt_ref[...] = _dot_nt(wiwt_ref[...], h)
    c128, s128 = c128_ref[...], s128_ref[...]
    c64, s64a, s64b = c64_ref[...], s64a_ref[...], s64b_ref[...]

    def rope128(t):
        return t * c128 + pltpu.roll(t, dh // 2, 1) * s128

    def rope64(t):
        return t * c64 + pltpu.roll(t, LANES - IDX_DIM // 2, 1) * s64a + pltpu.roll(t, IDX_DIM // 2, 1) * s64b

    for hh in range(N_HEADS):
        q_ref[:, hh * dh:(hh + 1) * dh] = _bf(rope128(_rms(main[:, hh * dh:(hh + 1) * dh], gq_ref[...])))
    o = bw
    for hh in range(N_KV_HEADS):
        kh = rope128(_rms(main[:, o + hh * dh:o + (hh + 1) * dh], gk_ref[...]))
        k_ref[:, hh * dh:(hh + 1) * dh] = kh
        kbf_ref[:, hh * dh:(hh + 1) * dh] = _bf(kh)
    o += kvw
    v = main[:, o:o + kvw]
    v_ref[...] = v
    for j in range(tm // kc):
        vt_ref[j] = _bf(v[j * kc:(j + 1) * kc, :].T)
    o += kvw
    for j in range(iqw // LANES):
        iq_ref[:, j * LANES:(j + 1) * LANES] = _bf(rope64(main[:, o + j * LANES:o + (j + 1) * LANES]))
    o += iqw
    sz_ref[...] = _silu(main[:, o:o + bw])
    lane = lax.broadcasted_iota(I32, (tm, LANES), 1)
    ms = jnp.sum(jnp.where(lane < IDX_DIM, small * small, 0.0), axis=-1, keepdims=True) * (1.0 / IDX_DIM)
    ikr = rope64(small * lax.rsqrt(ms + EPS) * gik_ref[...])[:, :IDX_DIM]
    ik_ref[...] = ikr
    ikbf_ref[...] = _bf(ikr)
    iw_ref[...] = small[:, IDX_DIM:IDX_DIM + IDX_HEADS]


def _proj_b(x, gn, wmain, wsmall, wiwt, gq, gk, gik, tabs, *, tm, kc):
    n, d = x.shape
    bw = N_HEADS * LANES
    kvw = N_KV_HEADS * LANES
    iqw = IDX_HEADS * IDX_DIM
    period = tabs[0].shape[0] // tm
    row_spec = lambda w: pl.BlockSpec((tm, w), lambda i: (i, 0))
    tab_spec = pl.BlockSpec((tm, LANES), lambda i: (i % period, 0))
    sds = jax.ShapeDtypeStruct
    outs = [
        (sds((n, bw), BF16), row_spec(bw)),
        (sds((n, kvw), F32), row_spec(kvw)),
        (sds((n, kvw), F32), row_spec(kvw)),
        (sds((n, IDX_DIM), F32), row_spec(IDX_DIM)),
        (sds((n, kvw), BF16), row_spec(kvw)),
        (sds((n // kc, kvw, kc), BF16), pl.BlockSpec((tm // kc, kvw, kc), lambda i: (i, 0, 0))),
        (sds((n, IDX_DIM), BF16), row_spec(IDX_DIM)),
        (sds((n, iqw), BF16), row_spec(iqw)),
        (sds((n, IDX_HEADS), F32), row_spec(IDX_HEADS)),
        (sds((IDX_HEADS, n), F32), pl.BlockSpec((IDX_HEADS, tm), lambda i: (0, i))),
        (sds((n, bw), F32), row_spec(bw)),
    ]
    return pl.pallas_call(
        functools.partial(_proj_b_kernel, kc=kc),
        grid=(n // tm,),
        in_specs=[row_spec(d), _const_spec((1, d)), _const_spec(wmain.shape), _const_spec(wsmall.shape),
                  _const_spec(wiwt.shape), _const_spec((1, LANES)), _const_spec((1, LANES)), _const_spec((1, LANES)),
                  tab_spec, tab_spec, tab_spec, tab_spec, tab_spec],
        out_specs=[o[1] for o in outs],
        out_shape=[o[0] for o in outs],
        compiler_params=_params(("arbitrary",)),
        name="proj_b",
    )(x, gn, wmain, wsmall, wiwt, gq, gk, gik, *tabs)


def _dsa_prompt_kernel(iq_ref, iwt_ref, ikbf_ref, kbf_ref, vt_ref, q_ref, sz_ref, x_ref, p_ref,
                       wout_ref, gpe_ref, wg_ref, wpe_ref, xo_ref, key_sc, m_sc, l_sc, acc_sc, *, topk):
    tq = q_ref.shape[0]
    s_len = key_sc.shape[0]
    dh = LANES
    i = pl.program_id(1)
    nch = i + 1
    qpos = i * tq + lax.broadcasted_iota(I32, (1, tq), 1)
    krow = lax.broadcasted_iota(I32, (tq, tq), 0)
    idx_scale = IDX_DIM ** -0.5 * IDX_HEADS ** -0.5

    def chunk_start(c):
        return pl.multiple_of(c * tq, tq)

    def build(c, carry):
        off = chunk_start(c)
        ikc = ikbf_ref[pl.ds(off, tq), :]
        s = jnp.zeros((tq, tq), F32)
        for h in range(IDX_HEADS):
            xh = _dot_nt(ikc, iq_ref[:, h * IDX_DIM:(h + 1) * IDX_DIM])
            s = s + jnp.maximum(xh, 0.0) * iwt_ref[h:h + 1, :]
        key = _sort_key(s * idx_scale)
        key_sc[pl.ds(off, tq), :] = jnp.where(off + krow <= qpos, key, INT_MIN)
        return carry

    lax.fori_loop(0, nch, build, 0)

    def count(pred):
        def body(c, acc):
            off = chunk_start(c)
            m = jnp.where(pred(key_sc[pl.ds(off, tq), :], off + krow), 1.0, 0.0)
            part = m[0:32]
            for r in range(1, tq // 32):
                part = part + m[r * 32:(r + 1) * 32]
            return acc + part
        acc = lax.fori_loop(0, nch, body, jnp.zeros((32, tq), F32))
        return jnp.sum(acc, axis=0, keepdims=True)

    kk = float(topk)

    def one_pass(j, carry):
        t, cnt_t = carry
        cand = t + (jnp.int32(1) << (31 - j))
        cnt = count(lambda blk, pos: blk >= cand)
        ok = cnt >= kk
        return jnp.where(ok, cand, t), jnp.where(ok, cnt, cnt_t)

    t_thr, cnt_thr = lax.fori_loop(0, 32, one_pass,
                                   (jnp.full((1, tq), INT_MIN, I32), jnp.zeros((1, tq), F32)))

    need = jnp.logical_and(cnt_thr > kk, t_thr > INT_MIN)

    @pl.when(jnp.max(jnp.where(need, 1.0, 0.0)) > 0.0)
    def _():
        quota = kk - count(lambda blk, pos: blk > t_thr)
        nbits = max(1, (s_len - 1).bit_length())

        def idx_pass(j, lim):
            cand = lim + (jnp.int32(1) << (nbits - 1 - j))
            cnt = count(lambda blk, pos: jnp.logical_and(blk == t_thr, pos < cand))
            return jnp.where(cnt < quota, cand, lim)

        last = lax.fori_loop(0, nbits, idx_pass, jnp.zeros((1, tq), I32))

        def demote(c, carry):
            off = chunk_start(c)
            blk = key_sc[pl.ds(off, tq), :]
            drop = jnp.logical_and(jnp.logical_and(blk == t_thr, off + krow > last), need)
            key_sc[pl.ds(off, tq), :] = jnp.where(drop, t_thr - 1, blk)
            return carry

        lax.fori_loop(0, nch, demote, 0)

    t_eff = jnp.maximum(t_thr, INT_MIN + 1)

    m_sc[...] = jnp.full(m_sc.shape, NEG, F32)
    l_sc[...] = jnp.zeros(l_sc.shape, F32)
    acc_sc[...] = jnp.zeros(acc_sc.shape, F32)
    scale = dh ** -0.5
    rep = N_HEADS // N_KV_HEADS

    def attend(c, carry):
        off = chunk_start(c)
        bias = jnp.where(key_sc[pl.ds(off, tq), :] >= t_eff, 0.0, NEG)
        kc_all = kbf_ref[pl.ds(off, tq), :]
        for g in range(N_KV_HEADS):
            kg = kc_all[:, g * dh:(g + 1) * dh]
            vg = vt_ref[c, g * dh:(g + 1) * dh, :]
            for r in range(rep):
                h = g * rep + r
                s = _dot_nt(kg, q_ref[:, h * dh:(h + 1) * dh]) * scale + bias
                m_old = m_sc[h:h + 1, :]
                m_new = jnp.maximum(m_old, jnp.max(s, axis=0, keepdims=True))
                p = jnp.exp(s - m_new)
                alpha = jnp.exp(m_old - m_new)
                l_sc[h:h + 1, :] = alpha * l_sc[h:h + 1, :] + jnp.sum(p, axis=0, keepdims=True)
                acc_sc[h * dh:(h + 1) * dh, :] = alpha * acc_sc[h * dh:(h + 1) * dh, :] + _dot(vg, _bf(p))
                m_sc[h:h + 1, :] = m_new
        return carry

    lax.fori_loop(0, nch, attend, 0)

    inv = 1.0 / l_sc[...]
    att_t = jnp.concatenate([acc_sc[h * dh:(h + 1) * dh, :] * inv[h:h + 1, :] for h in range(N_HEADS)], axis=0)
    y = _bf(att_t.T * sz_ref[...])
    x1 = x_ref[...] + _dot(y, wout_ref[...])
    xo_ref[...] = _per_layer_embed(x1, p_ref[...], gpe_ref[...], wg_ref[...], wpe_ref[...])


def _dsa_prompt(pr, x, p, wout, gpe, wg, wpe, *, bsz, s_len, tq, topk):
    q, _, _, _, kbf, vt, ikbf, iq, _, iwt, sz = pr
    n, d = x.shape
    bw = q.shape[1]
    kvw = kbf.shape[1]
    nq = s_len // tq
    blk = lambda w: pl.BlockSpec((tq, w), lambda b, i: (b * nq + i, 0))
    per_b = lambda w: pl.BlockSpec((s_len, w), lambda b, i: (b, 0))
    return pl.pallas_call(
        functools.partial(_dsa_prompt_kernel, topk=topk),
        grid=(bsz, nq),
        in_specs=[blk(iq.shape[1]),
                  pl.BlockSpec((IDX_HEADS, tq), lambda b, i: (0, b * nq + i)),
                  per_b(IDX_DIM), per_b(kvw),
                  pl.BlockSpec((nq, kvw, tq), lambda b, i: (b, 0, 0)),
                  blk(bw), blk(bw), blk(d), blk(p.shape[1]),
                  _const_spec(wout.shape), _const_spec((1, d)), _const_spec(wg.shape), _const_spec(wpe.shape)],
        out_specs=blk(d),
        out_shape=jax.ShapeDtypeStruct((n, d), F32),
        scratch_shapes=[pltpu.VMEM((s_len, tq), I32), pltpu.VMEM((N_HEADS, tq), F32),
                        pltpu.VMEM((N_HEADS, tq), F32), pltpu.VMEM((bw, tq), F32)],
        compiler_params=_params(("arbitrary", "arbitrary")),
        name="dsa_prompt",
    )(iq, iwt, ikbf, kbf, vt, q, sz, x, p, wout, gpe, wg, wpe)


def _page_copies(pt_ref, cache_ref, buf, sem, step, slot, *, group, n_pages, layer):
    copies = []
    for b in range(group):
        for n in range(n_pages):
            page = pt_ref[step * group + b, n]
            copies.append(pltpu.make_async_copy(cache_ref.at[page, layer],
                                                buf.at[slot, b, pl.ds(n * PAGE_SIZE, PAGE_SIZE), :],
                                                sem.at[slot]))
    return copies


def _paged_pipeline(make_copies_list):
    step = pl.program_id(0)
    nsteps = pl.num_programs(0)
    slot = step % 2

    @pl.when(step == 0)
    def _():
        for mk in make_copies_list:
            for cp in mk(step, slot):
                cp.start()

    @pl.when(step + 1 < nsteps)
    def _():
        for mk in make_copies_list:
            for cp in mk(step + 1, 1 - slot):
                cp.start()

    for mk in make_copies_list:
        for cp in mk(step, slot):
            cp.wait()
    return slot


def _idx_sample_kernel(pt_ref, iq_ref, iw_ref, iknew_ref, cik_ref, bias_ref, ikbuf, sem, key_sc,
                       *, group, n_pages, layer, t_new, topk):
    past = n_pages * PAGE_SIZE
    lp = past + LANES
    slot = _paged_pipeline([functools.partial(_page_copies, pt_ref, cik_ref, ikbuf, sem,
                                              group=group, n_pages=n_pages, layer=layer)])
    idx_scale = IDX_DIM ** -0.5 * IDX_HEADS ** -0.5
    iq_all = iq_ref[...].astype(F32)
    iknew_all = iknew_ref[...].astype(F32)
    iw_all = iw_ref[...]
    kpos = lax.broadcasted_iota(I32, (t_new, lp), 1)
    qpos = past + lax.broadcasted_iota(I32, (t_new, lp), 0)
    for b in range(group):
        rows = slice(b * t_new, (b + 1) * t_new)
        iqb = iq_all[rows]
        iqs = _bf(jnp.concatenate([iqb[:, h * IDX_DIM:(h + 1) * IDX_DIM] for h in range(IDX_HEADS)], axis=0))
        ik_past = _bf(ikbuf[slot, b])
        ik_new = _bf(jnp.concatenate([iknew_all[rows], jnp.zeros((LANES - t_new, IDX_DIM), F32)], axis=0))
        xp = _dot_nt(iqs, ik_past)
        xn = _dot_nt(iqs, ik_new)
        iwb = iw_all[rows]
        sp = jnp.zeros((t_new, past), F32)
        sn = jnp.zeros((t_new, LANES), F32)
        for h in range(IDX_HEADS):
            w = iwb[:, h:h + 1]
            sp = sp + jnp.maximum(xp[h * t_new:(h + 1) * t_new], 0.0) * w
            sn = sn + jnp.maximum(xn[h * t_new:(h + 1) * t_new], 0.0) * w
        key = _sort_key(jnp.concatenate([sp, sn], axis=1) * idx_scale)
        key_sc[rows, :] = jnp.where(kpos <= qpos, key, INT_MIN)

    kk = float(topk)
    rows_all = group * t_new
    kpos_all = lax.broadcasted_iota(I32, (rows_all, lp), 1)

    def count(pred):
        return jnp.sum(jnp.where(pred(key_sc[...]), 1.0, 0.0), axis=1, keepdims=True)

    def one_pass(j, carry):
        t, cnt_t = carry
        cand = t + (jnp.int32(1) << (31 - j))
        cnt = count(lambda blk: blk >= cand)
        ok = cnt >= kk
        return jnp.where(ok, cand, t), jnp.where(ok, cnt, cnt_t)

    t_thr, cnt_thr = lax.fori_loop(0, 32, one_pass,
                                   (jnp.full((rows_all, 1), INT_MIN, I32), jnp.zeros((rows_all, 1), F32)))
    need = jnp.logical_and(cnt_thr > kk, t_thr > INT_MIN)

    @pl.when(jnp.max(jnp.where(need, 1.0, 0.0)) > 0.0)
    def _():
        quota = kk - count(lambda blk: blk > t_thr)
        nbits = max(1, (lp - 1).bit_length())

        def idx_pass(j, lim):
            cand = lim + (jnp.int32(1) << (nbits - 1 - j))
            cnt = count(lambda blk: jnp.logical_and(blk == t_thr, kpos_all < cand))
            return jnp.where(cnt < quota, cand, lim)

        last = lax.fori_loop(0, nbits, idx_pass, jnp.zeros((rows_all, 1), I32))
        blk = key_sc[...]
        drop = jnp.logical_and(jnp.logical_and(blk == t_thr, kpos_all > last), need)
        key_sc[...] = jnp.where(drop, t_thr - 1, blk)

    t_eff = jnp.maximum(t_thr, INT_MIN + 1)
    bias_ref[...] = jnp.where(key_sc[...] >= t_eff, 0.0, NEG)


def _idx_sample(page_table, iq, iw, ikbf, cache_ik, *, layer, group, t_new, topk):
    n = iq.shape[0]
    db, n_pages = page_table.shape
    past = n_pages * PAGE_SIZE
    lp = past + LANES
    rows = group * t_new
    blk = lambda w: pl.BlockSpec((rows, w), lambda i, pt: (i, 0))
    return pl.pallas_call(
        functools.partial(_idx_sample_kernel, group=group, n_pages=n_pages, layer=layer, t_new=t_new, topk=topk),
        grid_spec=pltpu.PrefetchScalarGridSpec(
            num_scalar_prefetch=1,
            grid=(db // group,),
            in_specs=[blk(iq.shape[1]), blk(IDX_HEADS), blk(IDX_DIM), pl.BlockSpec(memory_space=pl.ANY)],
            out_specs=blk(lp),
            scratch_shapes=[pltpu.VMEM((2, group, past, IDX_DIM), F32), pltpu.SemaphoreType.DMA((2,)),
                            pltpu.VMEM((rows, lp), I32)]),
        out_shape=jax.ShapeDtypeStruct((n, lp), F32),
        compiler_params=_params(("arbitrary",)),
        name="idx_sample",
    )(page_table, iq, iw, ikbf, cache_ik)


def _attn_sample_kernel(pt_ref, q_ref, bias_ref, knew_ref, vnew_ref, sz_ref, ck_ref, cv_ref, y_ref,
                        kbuf, vbuf, sem, *, group, n_pages, layer, t_new):
    past = n_pages * PAGE_SIZE
    dh = LANES
    kvw = N_KV_HEADS * dh
    rep = N_HEADS // N_KV_HEADS
    slot = _paged_pipeline([
        functools.partial(_page_copies, pt_ref, ck_ref, kbuf, sem.at[0], group=group, n_pages=n_pages, layer=layer),
        functools.partial(_page_copies, pt_ref, cv_ref, vbuf, sem.at[1], group=group, n_pages=n_pages, layer=layer)])
    scale = dh ** -0.5
    q_all = q_ref[...].astype(F32)
    pad = jnp.zeros((LANES - t_new, kvw), F32)
    for b in range(group):
        rows = slice(b * t_new, (b + 1) * t_new)
        kbuf[slot, b, pl.ds(past, LANES), :] = jnp.concatenate([knew_ref[rows, :], pad], axis=0)
        vbuf[slot, b, pl.ds(past, LANES), :] = jnp.concatenate([vnew_ref[rows, :], pad], axis=0)
        kb = _bf(kbuf[slot, b])
        vb = _bf(vbuf[slot, b])
        qb = q_all[rows]
        bias = jnp.concatenate([bias_ref[rows, :]] * rep, axis=0)
        outs = []
        for g in range(N_KV_HEADS):
            qg = _bf(jnp.concatenate([qb[:, (g * rep + r) * dh:(g * rep + r + 1) * dh] for r in range(rep)], axis=0))
            s = _dot_nt(qg, kb[:, g * dh:(g + 1) * dh]) * scale + bias
            m = jnp.max(s, axis=1, keepdims=True)
            p = jnp.exp(s - m)
            l = jnp.sum(p, axis=1, keepdims=True)
            o = _dot(_bf(p), vb[:, g * dh:(g + 1) * dh]) * (1.0 / l)
            outs.extend(o[r * t_new:(r + 1) * t_new] for r in range(rep))
        y_ref[rows, :] = jnp.concatenate(outs, axis=1) * sz_ref[rows, :]


def _attn_sample(page_table, q, bias, knew, vnew, sz, cache_k, cache_v, *, layer, group, t_new):
    n, bw = q.shape
    db, n_pages = page_table.shape
    past = n_pages * PAGE_SIZE
    lp = past + LANES
    kvw = knew.shape[1]
    rows = group * t_new
    blk = lambda w: pl.BlockSpec((rows, w), lambda i, pt: (i, 0))
    return pl.pallas_call(
        functools.partial(_attn_sample_kernel, group=group, n_pages=n_pages, layer=layer, t_new=t_new),
        grid_spec=pltpu.PrefetchScalarGridSpec(
            num_scalar_prefetch=1,
            grid=(db // group,),
            in_specs=[blk(bw), blk(lp), blk(kvw), blk(kvw), blk(bw),
                      pl.BlockSpec(memory_space=pl.ANY), pl.BlockSpec(memory_space=pl.ANY)],
            out_specs=blk(bw),
            scratch_shapes=[pltpu.VMEM((2, group, lp, kvw), F32), pltpu.VMEM((2, group, lp, kvw), F32),
                            pltpu.SemaphoreType.DMA((2, 2))]),
        out_shape=jax.ShapeDtypeStruct((n, bw), F32),
        compiler_params=_params(("arbitrary",)),
        name="attn_sample",
    )(page_table, q, bias, knew, vnew, sz, cache_k, cache_v)


def _out_ple_kernel(y_ref, x_ref, p_ref, wout_ref, gpe_ref, wg_ref, wpe_ref, xo_ref):
    x1 = x_ref[...] + _dot(_bf(y_ref[...]), wout_ref[...])
    xo_ref[...] = _per_layer_embed(x1, p_ref[...], gpe_ref[...], wg_ref[...], wpe_ref[...])


def _out_ple(y, x, p, wout, gpe, wg, wpe, *, tm):
    n, d = x.shape
    row_spec = lambda w: pl.BlockSpec((tm, w), lambda i: (i, 0))
    return pl.pallas_call(
        _out_ple_kernel,
        grid=(n // tm,),
        in_specs=[row_spec(y.shape[1]), row_spec(d), row_spec(p.shape[1]),
                  _const_spec(wout.shape), _const_spec((1, d)), _const_spec(wg.shape), _const_spec(wpe.shape)],
        out_specs=row_spec(d),
        out_shape=jax.ShapeDtypeStruct((n, d), F32),
        compiler_params=_params(("arbitrary",)),
        name="out_ple",
    )(y, x, p, wout, gpe, wg, wpe)


def _rope_tables(pos):
    def angles(half):
        inv = ROPE_THETA ** (-jnp.arange(half, dtype=F32) / half)
        return pos.astype(F32)[:, None] * inv[None, :]
    a = angles(LANES // 2)
    c128 = jnp.concatenate([jnp.cos(a), jnp.cos(a)], axis=1)
    s128 = jnp.concatenate([-jnp.sin(a), jnp.sin(a)], axis=1)
    a = angles(IDX_DIM // 2)
    zero = jnp.zeros_like(a)
    c64 = jnp.tile(jnp.cos(a), (1, 4))
    s64a = jnp.tile(jnp.concatenate([-jnp.sin(a), zero], axis=1), (1, 2))
    s64b = jnp.tile(jnp.concatenate([zero, jnp.sin(a)], axis=1), (1, 2))
    return c128, s128, c64, s64a, s64b


def _tile_rows(n, pref):
    t = min(n, pref)
    assert n % t == 0, (n, t)
    return t


def kernel(x_prompt, x_sample, cache_k, cache_v, cache_ik, page_table, p_prompt, p_sample, g_norm, w_in_a, w_s_a, b_s_a, g_v_a, w_out_a, w_in_b, g_q_b, g_k_b, g_ik_b, w_out_b, g_pe, w_pe_gate, w_pe_proj):
    bsz, s_len, d = x_prompt.shape
    db, t_new, _ = x_sample.shape
    depth = g_norm.shape[0]
    n_pool, nb_layers = cache_k.shape[0], cache_k.shape[1]
    n_pages = page_table.shape[1]
    past = n_pages * PAGE_SIZE
    aw = g_v_a.shape[1]
    bw = N_HEADS * LANES
    kvw = N_KV_HEADS * LANES
    iqw = IDX_HEADS * IDX_DIM
    assert d == bw and s_len % CHUNK == 0 and CHUNK % t_new == 0 and t_new % 8 == 0
    assert w_in_b.shape[2] == 2 * bw + 2 * kvw + iqw + IDX_DIM + IDX_HEADS

    n_p, n_s = bsz * s_len, db * t_new
    yp = x_prompt.reshape(n_p, d)
    ys = x_sample.reshape(n_s, d)
    pp = p_prompt.reshape(depth, n_p, -1)
    ps = p_sample.reshape(depth, n_s, -1)
    ck = cache_k.reshape(n_pool, nb_layers, PAGE_SIZE, kvw)
    cv = cache_v.reshape(n_pool, nb_layers, PAGE_SIZE, kvw)

    tm_p = _tile_rows(n_p, 256)
    tm_s = _tile_rows(n_s, 256)
    tq = _tile_rows(s_len, 256)
    topk_p = min(TOPK_MAX, s_len // 4)
    topk_s = min(TOPK_MAX, (past + t_new) // 4)
    idx_group = _tile_rows(db, 8)
    att_group = _tile_rows(db, 2)

    tabs_p = _rope_tables(jnp.arange(s_len))
    tabs_s = _rope_tables(past + (jnp.arange(tm_s) % t_new))

    gn = g_norm.reshape(depth, 1, d)
    gpe = g_pe.reshape(depth, 1, d)
    wg = _bf(w_pe_gate)
    wpe = _bf(w_pe_proj)

    kp_l, vp_l, ikp_l, ks_l, vs_l, iks_l, cv_l = [], [], [], [], [], [], []
    for i in range(depth):
        j = i // 2
        if i % 2 == 0:
            win = _bf(w_in_a[j])
            wout = _bf(w_out_a[j])
            gv = g_v_a[j].reshape(1, aw)
            reps = CHUNK // t_new
            ws_p = w_s_a[j]
            bs_p = jnp.repeat(b_s_a[j].T, aw // A_GROUPS, axis=1)
            ws_s = jnp.tile(w_s_a[j][:, :t_new, :t_new], (1, reps, reps))
            bs_s = jnp.tile(jnp.repeat(b_s_a[j][:, :t_new].T, aw // A_GROUPS, axis=1), (reps, 1))
            yp, _ = _layer_a(yp, pp[i], gn[i], win, ws_p, bs_p, gv, wout, gpe[i], wg[i], wpe[i],
                             chunk=CHUNK, tm=tm_p, want_v=False)
            ys, v_rows = _layer_a(ys, ps[i], gn[i], win, ws_s, bs_s, gv, wout, gpe[i], wg[i], wpe[i],
                                  chunk=t_new, tm=tm_s, want_v=True)
            cv_l.append(v_rows.reshape(db, t_new, aw))
        else:
            w = w_in_b[j]
            o_iq = bw + 2 * kvw
            o_ik = o_iq + iqw
            o_iw = o_ik + IDX_DIM
            o_z = o_iw + IDX_HEADS
            wmain = _bf(jnp.concatenate([w[:, :o_ik], w[:, o_z:]], axis=1))
            wsmall = _bf(jnp.pad(w[:, o_ik:o_z], ((0, 0), (0, LANES - IDX_DIM - IDX_HEADS))))
            wiwt = _bf(w[:, o_iw:o_z].T)
            gq = g_q_b[j].reshape(1, LANES)
            gk = g_k_b[j].reshape(1, LANES)
            gik = jnp.pad(g_ik_b[j], (0, LANES - IDX_DIM)).reshape(1, LANES)
            wout = _bf(w_out_b[j])

            pr = _proj_b(yp, gn[i], wmain, wsmall, wiwt, gq, gk, gik, tabs_p, tm=tm_p, kc=tq)
            kp_l.append(pr[1].reshape(bsz, s_len, N_KV_HEADS, LANES))
            vp_l.append(pr[2].reshape(bsz, s_len, N_KV_HEADS, LANES))
            ikp_l.append(pr[3].reshape(bsz, s_len, IDX_DIM))
            yp = _dsa_prompt(pr, yp, pp[i], wout, gpe[i], wg[i], wpe[i], bsz=bsz, s_len=s_len, tq=tq, topk=topk_p)

            sr = _proj_b(ys, gn[i], wmain, wsmall, wiwt, gq, gk, gik, tabs_s, tm=tm_s, kc=min(tm_s, LANES))
            q_s, k_s, v_s, ik_s, _, _, ikbf_s, iq_s, iw_s, _, sz_s = sr
            ks_l.append(k_s.reshape(db, t_new, N_KV_HEADS, LANES))
            vs_l.append(v_s.reshape(db, t_new, N_KV_HEADS, LANES))
            iks_l.append(ik_s.reshape(db, t_new, IDX_DIM))
            bias = _idx_sample(page_table, iq_s, iw_s, ikbf_s, cache_ik, layer=j, group=idx_group, t_new=t_new, topk=topk_s)
            y_s = _attn_sample(page_table, q_s, bias, k_s, v_s, sz_s, ck, cv, layer=j, group=att_group, t_new=t_new)
            ys = _out_ple(y_s, ys, ps[i], wout, gpe[i], wg[i], wpe[i], tm=tm_s)

    return (yp.reshape(bsz, s_len, d), ys.reshape(db, t_new, d),
            jnp.stack(kp_l, axis=1), jnp.stack(vp_l, axis=1), jnp.stack(ikp_l, axis=1),
            jnp.stack(ks_l, axis=1), jnp.stack(vs_l, axis=1), jnp.stack(iks_l, axis=1),
            jnp.stack(cv_l, axis=1))
```

```python
import functools

import jax
import jax.numpy as jnp
from jax import lax
from jax.experimental import pallas as pl
from jax.experimental.pallas import tpu as pltpu

EPS = 1e-6
ROPE_THETA = 10000.0
CHUNK = 128
A_GROUPS = 8
N_HEADS = 8
N_KV_HEADS = 2
IDX_HEADS = 8
IDX_DIM = 64
TOPK_MAX = 256
PAGE_SIZE = 128

LANES = 128
SUBLANES = 8
INT_MIN = -(2 ** 31)
NEG = -1e30
VMEM_LIMIT = 56 * 1024 * 1024

F32 = jnp.float32
BF16 = jnp.bfloat16
I32 = jnp.int32


def _bf(x):
    return x.astype(BF16)


def _dot(a, b):
    return jnp.dot(a, b, preferred_element_type=F32)


def _dot_nt(a, b):
    return lax.dot_general(a, b, (((1,), (1,)), ((), ())), preferred_element_type=F32)


def _rms(x, g):
    r = lax.rsqrt(jnp.mean(x * x, axis=-1, keepdims=True) + EPS)
    return x * r * g


def _silu(z):
    return z * jax.nn.sigmoid(z)


def _per_layer_embed(x1, p, gpe, wg, wpe):
    gate = jax.nn.sigmoid(_dot(_bf(_rms(x1, gpe)), wg))
    return x1 + gate * _dot(_bf(p), wpe)


def _sort_key(s):
    bits = pltpu.bitcast(s, I32)
    mag = bits & jnp.int32(0x7FFFFFFF)
    return jnp.where(bits < 0, -mag, bits)


def _const_spec(shape):
    nd = len(shape)
    return pl.BlockSpec(shape, lambda *_: (0,) * nd, pipeline_mode=pl.Buffered(1))


def _params(sem):
    return pltpu.CompilerParams(dimension_semantics=sem, vmem_limit_bytes=VMEM_LIMIT)


def _layer_a_kernel(x_ref, p_ref, gn_ref, win_ref, ws_ref, bs_ref, gv_ref, wout_ref,
                    gpe_ref, wg_ref, wpe_ref, xo_ref, *v_out, chunk):
    aw = gv_ref.shape[1]
    gd = aw // A_GROUPS
    x = x_ref[...]
    tm = x.shape[0]
    h = _bf(_rms(x, gn_ref[...]))
    u = _dot(h, win_ref[:, 0:aw])
    v = _rms(_dot(h, win_ref[:, aw:2 * aw]), gv_ref[...])
    z = _dot(h, win_ref[:, 2 * aw:3 * aw])
    if v_out:
        v_out[0][...] = v
    vb = _bf(v)
    row = lax.broadcasted_iota(I32, (CHUNK, CHUNK), 0)
    col = lax.broadcasted_iota(I32, (CHUNK, CHUNK), 1)
    keep = col <= row
    if chunk != CHUNK:
        keep = jnp.logical_and(keep, (row // chunk) == (col // chunk))
    cols = []
    for g in range(A_GROUPS):
        wsg = _bf(jnp.where(keep, ws_ref[g], 0.0))
        rows = [_dot(wsg, vb[j * CHUNK:(j + 1) * CHUNK, g * gd:(g + 1) * gd]) for j in range(tm // CHUNK)]
        cols.append(rows[0] if len(rows) == 1 else jnp.concatenate(rows, axis=0))
    mixed = jnp.concatenate(cols, axis=1)
    bias = bs_ref[...]
    if tm != CHUNK:
        bias = jnp.concatenate([bias] * (tm // CHUNK), axis=0)
    y = u * (mixed + bias) * _silu(z)
    x1 = x + _dot(_bf(y), wout_ref[...])
    xo_ref[...] = _per_layer_embed(x1, p_ref[...], gpe_ref[...], wg_ref[...], wpe_ref[...])


def _layer_a(x, p, gn, win, ws_t, bs_t, gv, wout, gpe, wg, wpe, *, chunk, tm, want_v):
    n, d = x.shape
    aw = gv.shape[1]
    pd = p.shape[1]
    row_spec = lambda w: pl.BlockSpec((tm, w), lambda i: (i, 0))
    out_shape = [jax.ShapeDtypeStruct((n, d), F32)]
    out_specs = [row_spec(d)]
    if want_v:
        out_shape.append(jax.ShapeDtypeStruct((n, aw), F32))
        out_specs.append(row_spec(aw))
    res = pl.pallas_call(
        functools.partial(_layer_a_kernel, chunk=chunk),
        grid=(n // tm,),
        in_specs=[row_spec(d), row_spec(pd), _const_spec((1, d)), _const_spec((d, 3 * aw)),
                  _const_spec((A_GROUPS, CHUNK, CHUNK)), _const_spec((CHUNK, aw)), _const_spec((1, aw)),
                  _const_spec((aw, d)), _const_spec((1, d)), _const_spec((d, d)), _const_spec((pd, d))],
        out_specs=out_specs,
        out_shape=out_shape,
        compiler_params=_params(("arbitrary",)),
        name="layer_a",
    )(x, p, gn, win, ws_t, bs_t, gv, wout, gpe, wg, wpe)
    return res if want_v else (res[0], None)


def _proj_b_kernel(x_ref, gn_ref, wmain_ref, wsmall_ref, wiwt_ref, gq_ref, gk_ref, gik_ref,
                   c128_ref, s128_ref, c64_ref, s64a_ref, s64b_ref,
                   q_ref, k_ref, v_ref, ik_ref, kbf_ref, vt_ref, ikbf_ref, iq_ref, iw_ref, iwt_ref, sz_ref, *, kc):
    dh = LANES
    bw = N_HEADS * dh
    kvw = N_KV_HEADS * dh
    iqw = IDX_HEADS * IDX_DIM
    x = x_ref[...]
    tm = x.shape[0]
    h = _bf(_rms(x, gn_ref[...]))
    main = _dot(h, wmain_ref[...])
    small = _dot(h, wsmall_ref[...])
    iwt_ref[...] = _dot_nt(wiwt_ref[...], h)
    c128, s128 = c128_ref[...], s128_ref[...]
    c64, s64a, s64b = c64_ref[...], s64a_ref[...], s64b_ref[...]

    def rope128(t):
        return t * c128 + pltpu.roll(t, dh // 2, 1) * s128

    def rope64(t):
        return t * c64 + pltpu.roll(t, LANES - IDX_DIM // 2, 1) * s64a + pltpu.roll(t, IDX_DIM // 2, 1) * s64b

    for hh in range(N_HEADS):
        q_ref[:, hh * dh:(hh + 1) * dh] = _bf(rope128(_rms(main[:, hh * dh:(hh + 1) * dh], gq_ref[...])))
    o = bw
    for hh in range(N_KV_HEADS):
        kh = rope128(_rms(main[:, o + hh * dh:o + (hh + 1) * dh], gk_ref[...]))
        k_ref[:, hh * dh:(hh + 1) * dh] = kh
        kbf_ref[:, hh * dh:(hh + 1) * dh] = _bf(kh)
    o += kvw
    v = main[:, o:o + kvw]
    v_ref[...] = v
    for j in range(tm // kc):
        vt_ref[j] = _bf(v[j * kc:(j + 1) * kc, :].T)
    o += kvw
    for j in range(iqw // LANES):
        iq_ref[:, j * LANES:(j + 1) * LANES] = _bf(rope64(main[:, o + j * LANES:o + (j + 1) * LANES]))
    o += iqw
    sz_ref[...] = _silu(main[:, o:o + bw])
    lane = lax.broadcasted_iota(I32, (tm, LANES), 1)
    ms = jnp.sum(jnp.where(lane < IDX_DIM, small * small, 0.0), axis=-1, keepdims=True) * (1.0 / IDX_DIM)
    ikr = rope64(small * lax.rsqrt(ms + EPS) * gik_ref[...])[:, :IDX_DIM]
    ik_ref[...] = ikr
    ikbf_ref[...] = _bf(ikr)
    iw_ref[...] = small[:, IDX_DIM:IDX_DIM + IDX_HEADS]


def _proj_b(x, gn, wmain, wsmall, wiwt, gq, gk, gik, tabs, *, tm, kc):
    n, d = x.shape
    bw = N_HEADS * LANES
    kvw = N_KV_HEADS * LANES
    iqw = IDX_HEADS * IDX_DIM
    period = tabs[0].shape[0] // tm
    row_spec = lambda w: pl.BlockSpec((tm, w), lambda i: (i, 0))
    tab_spec = pl.BlockSpec((tm, LANES), lambda i: (i % period, 0))
    sds = jax.ShapeDtypeStruct
    outs = [
        (sds((n, bw), BF16), row_spec(bw)),
        (sds((n, kvw), F32), row_spec(kvw)),
        (sds((n, kvw), F32), row_spec(kvw)),
        (sds((n, IDX_DIM), F32), row_spec(IDX_DIM)),
        (sds((n, kvw), BF16), row_spec(kvw)),
        (sds((n // kc, kvw, kc), BF16), pl.BlockSpec((tm // kc, kvw, kc), lambda i: (i, 0, 0))),
        (sds((n, IDX_DIM), BF16), row_spec(IDX_DIM)),
        (sds((n, iqw), BF16), row_spec(iqw)),
        (sds((n, IDX_HEADS), F32), row_spec(IDX_HEADS)),
        (sds((IDX_HEADS, n), F32), pl.BlockSpec((IDX_HEADS, tm), lambda i: (0, i))),
        (sds((n, bw), F32), row_spec(bw)),
    ]
    return pl.pallas_call(
        functools.partial(_proj_b_kernel, kc=kc),
        grid=(n // tm,),
        in_specs=[row_spec(d), _const_spec((1, d)), _const_spec(wmain.shape), _const_spec(wsmall.shape),
                  _const_spec(wiwt.shape), _const_spec((1, LANES)), _const_spec((1, LANES)), _const_spec((1, LANES)),
                  tab_spec, tab_spec, tab_spec, tab_spec, tab_spec],
        out_specs=[o[1] for o in outs],
        out_shape=[o[0] for o in outs],
        compiler_params=_params(("arbitrary",)),
        name="proj_b",
    )(x, gn, wmain, wsmall, wiwt, gq, gk, gik, *tabs)


def _dsa_prompt_kernel(iq_ref, iwt_ref, ikbf_ref, kbf_ref, vt_ref, q_ref, sz_ref, x_ref, p_ref,
                       wout_ref, gpe_ref, wg_ref, wpe_ref, xo_ref, key_sc, m_sc, l_sc, acc_sc, *, topk):
    tq = q_ref.shape[0]
    s_len = key_sc.shape[0]
    dh = LANES
    i = pl.program_id(1)
    nch = i + 1
    qpos = i * tq + lax.broadcasted_iota(I32, (1, tq), 1)
    krow = lax.broadcasted_iota(I32, (tq, tq), 0)
    idx_scale = IDX_DIM ** -0.5 * IDX_HEADS ** -0.5

    def chunk_start(c):
        return pl.multiple_of(c * tq, tq)

    def build(c, carry):
        off = chunk_start(c)
        ikc = ikbf_ref[pl.ds(off, tq), :]
        s = jnp.zeros((tq, tq), F32)
        for h in range(IDX_HEADS):
            xh = _dot_nt(ikc, iq_ref[:, h * IDX_DIM:(h + 1) * IDX_DIM])
            s = s + jnp.maximum(xh, 0.0) * iwt_ref[h:h + 1, :]
        key = _sort_key(s * idx_scale)
        key_sc[pl.ds(off, tq), :] = jnp.where(off + krow <= qpos, key, INT_MIN)
        return carry

    lax.fori_loop(0, nch, build, 0)

    def count(pred):
        def body(c, acc):
            off = chunk_start(c)
            m = jnp.where(pred(key_sc[pl.ds(off, tq), :], off + krow), 1.0, 0.0)
            part = m[0:32]
            for r in range(1, tq // 32):
                part = part + m[r * 32:(r + 1) * 32]
            return acc + part
        acc = lax.fori_loop(0, nch, body, jnp.zeros((32, tq), F32))
        return jnp.sum(acc, axis=0, keepdims=True)

    kk = float(topk)

    def one_pass(j, carry):
        t, cnt_t = carry
        cand = t + (jnp.int32(1) << (31 - j))
        cnt = count(lambda blk, pos: blk >= cand)
        ok = cnt >= kk
        return jnp.where(ok, cand, t), jnp.where(ok, cnt, cnt_t)

    t_thr, cnt_thr = lax.fori_loop(0, 32, one_pass,
                                   (jnp.full((1, tq), INT_MIN, I32), jnp.zeros((1, tq), F32)))

    need = jnp.logical_and(cnt_thr > kk, t_thr > INT_MIN)

    @pl.when(jnp.max(jnp.where(need, 1.0, 0.0)) > 0.0)
    def _():
        quota = kk - count(lambda blk, pos: blk > t_thr)
        nbits = max(1, (s_len - 1).bit_length())

        def idx_pass(j, lim):
            cand = lim + (jnp.int32(1) << (nbits - 1 - j))
            cnt = count(lambda blk, pos: jnp.logical_and(blk == t_thr, pos < cand))
            return jnp.where(cnt < quota, cand, lim)

        last = lax.fori_loop(0, nbits, idx_pass, jnp.zeros((1, tq), I32))

        def demote(c, carry):
            off = chunk_start(c)
            blk = key_sc[pl.ds(off, tq), :]
            drop = jnp.logical_and(jnp.logical_and(blk == t_thr, off + krow > last), need)
            key_sc[pl.ds(off, tq), :] = jnp.where(drop, t_thr - 1, blk)
            return carry

        lax.fori_loop(0, nch, demote, 0)

    t_eff = jnp.maximum(t_thr, INT_MIN + 1)

    m_sc[...] = jnp.full(m_sc.shape, NEG, F32)
    l_sc[...] = jnp.zeros(l_sc.shape, F32)
    acc_sc[...] = jnp.zeros(acc_sc.shape, F32)
    scale = dh ** -0.5
    rep = N_HEADS // N_KV_HEADS
    sub = SUBLANES

    def attend(c, carry):
        off = chunk_start(c)
        bias = jnp.where(key_sc[pl.ds(off, tq), :] >= t_eff, 0.0, NEG)
        kc_all = kbf_ref[pl.ds(off, tq), :]
        qk = [_dot_nt(kc_all[:, (h // rep) * dh:(h // rep + 1) * dh], q_ref[:, h * dh:(h + 1) * dh])
              for h in range(N_HEADS)]
        for g in range(N_KV_HEADS):
            vg = vt_ref[c, g * dh:(g + 1) * dh, :]
            for r in range(rep):
                h = g * rep + r
                st = slice(h * sub, (h + 1) * sub)
                s = (qk[h] * scale + bias).reshape(tq // sub, sub, tq)
                m_old = m_sc[st, :]
                m_new = jnp.maximum(m_old, jnp.max(jnp.max(s, axis=0), axis=0, keepdims=True))
                p = jnp.exp(s - m_new[None])
                alpha = jnp.exp(m_old - m_new)
                l_sc[st, :] = alpha * l_sc[st, :] + jnp.sum(p, axis=0)
                acc_sc[h * dh:(h + 1) * dh, :] = (alpha[0:1] * acc_sc[h * dh:(h + 1) * dh, :]
                                                  + _dot(vg, _bf(p.reshape(tq, tq))))
                m_sc[st, :] = m_new
        return carry

    lax.fori_loop(0, nch, attend, 0)

    inv = [1.0 / jnp.sum(l_sc[h * sub:(h + 1) * sub, :], axis=0, keepdims=True) for h in range(N_HEADS)]
    att_t = jnp.concatenate([acc_sc[h * dh:(h + 1) * dh, :] * inv[h] for h in range(N_HEADS)], axis=0)
    y = _bf(att_t.T * sz_ref[...])
    x1 = x_ref[...] + _dot(y, wout_ref[...])
    xo_ref[...] = _per_layer_embed(x1, p_ref[...], gpe_ref[...], wg_ref[...], wpe_ref[...])


def _dsa_prompt(pr, x, p, wout, gpe, wg, wpe, *, bsz, s_len, tq, topk):
    q, _, _, _, kbf, vt, ikbf, iq, _, iwt, sz = pr
    n, d = x.shape
    bw = q.shape[1]
    kvw = kbf.shape[1]
    nq = s_len // tq
    blk = lambda w: pl.BlockSpec((tq, w), lambda b, i: (b * nq + i, 0))
    per_b = lambda w: pl.BlockSpec((s_len, w), lambda b, i: (b, 0))
    return pl.pallas_call(
        functools.partial(_dsa_prompt_kernel, topk=topk),
        grid=(bsz, nq),
        in_specs=[blk(iq.shape[1]),
                  pl.BlockSpec((IDX_HEADS, tq), lambda b, i: (0, b * nq + i)),
                  per_b(IDX_DIM), per_b(kvw),
                  pl.BlockSpec((nq, kvw, tq), lambda b, i: (b, 0, 0)),
                  blk(bw), blk(bw), blk(d), blk(p.shape[1]),
                  _const_spec(wout.shape), _const_spec((1, d)), _const_spec(wg.shape), _const_spec(wpe.shape)],
        out_specs=blk(d),
        out_shape=jax.ShapeDtypeStruct((n, d), F32),
        scratch_shapes=[pltpu.VMEM((s_len, tq), I32), pltpu.VMEM((N_HEADS * SUBLANES, tq), F32),
                        pltpu.VMEM((N_HEADS * SUBLANES, tq), F32), pltpu.VMEM((bw, tq), F32)],
        compiler_params=_params(("arbitrary", "arbitrary")),
        name="dsa_prompt",
    )(iq, iwt, ikbf, kbf, vt, q, sz, x, p, wout, gpe, wg, wpe)


def _page_copies(pt_ref, cache_ref, buf, sem, step, slot, *, group, n_pages, layer, rows_per_page, axis):
    copies = []
    for b in range(group):
        for n in range(n_pages):
            page = pt_ref[step * group + b, n]
            span = pl.ds(n * rows_per_page, rows_per_page)
            dst = buf.at[slot, b, span, :] if axis == 0 else buf.at[slot, b, :, span]
            copies.append(pltpu.make_async_copy(cache_ref.at[page, layer], dst, sem.at[slot]))
    return copies


def _paged_pipeline(make_copies_list):
    step = pl.program_id(0)
    nsteps = pl.num_programs(0)
    slot = step % 2

    @pl.when(step == 0)
    def _():
        for mk in make_copies_list:
            for cp in mk(step, slot):
                cp.start()

    @pl.when(step + 1 < nsteps)
    def _():
        for mk in make_copies_list:
            for cp in mk(step + 1, 1 - slot):
                cp.start()

    for mk in make_copies_list:
        for cp in mk(step, slot):
            cp.wait()
    return slot


def _idx_sample_kernel(pt_ref, iq_ref, iw_ref, iknew_ref, cik_ref, bias_ref, ikbuf, sem, key_sc,
                       *, group, n_pages, layer, t_new, topk):
    past = n_pages * PAGE_SIZE
    lp = past + LANES
    slot = _paged_pipeline([functools.partial(_page_copies, pt_ref, cik_ref, ikbuf, sem, group=group,
                                              n_pages=n_pages, layer=layer, rows_per_page=PAGE_SIZE, axis=1)])
    idx_scale = IDX_DIM ** -0.5 * IDX_HEADS ** -0.5
    iq_all = iq_ref[...].astype(F32)
    iknew_all = iknew_ref[...].astype(F32)
    iw_all = iw_ref[...]
    kpos = lax.broadcasted_iota(I32, (t_new, lp), 1)
    qpos = past + lax.broadcasted_iota(I32, (t_new, lp), 0)
    for b in range(group):
        rows = slice(b * t_new, (b + 1) * t_new)
        iqb = iq_all[rows]
        iqs = _bf(jnp.concatenate([iqb[:, h * IDX_DIM:(h + 1) * IDX_DIM] for h in range(IDX_HEADS)], axis=0))
        ik_new = _bf(jnp.concatenate([iknew_all[rows], jnp.zeros((LANES - t_new, IDX_DIM), F32)], axis=0))
        xp = _dot(iqs, _bf(ikbuf[slot, b]))
        xn = _dot_nt(iqs, ik_new)
        iwb = iw_all[rows]
        sp = jnp.zeros((t_new, past), F32)
        sn = jnp.zeros((t_new, LANES), F32)
        for h in range(IDX_HEADS):
            w = iwb[:, h:h + 1]
            sp = sp + jnp.maximum(xp[h * t_new:(h + 1) * t_new], 0.0) * w
            sn = sn + jnp.maximum(xn[h * t_new:(h + 1) * t_new], 0.0) * w
        key = _sort_key(jnp.concatenate([sp, sn], axis=1) * idx_scale)
        key_sc[rows, :] = jnp.where(kpos <= qpos, key, INT_MIN)

    kk = float(topk)
    rows_all = group * t_new
    kpos_all = lax.broadcasted_iota(I32, (rows_all, lp), 1)

    def count(pred):
        return jnp.sum(jnp.where(pred(key_sc[...]), 1.0, 0.0), axis=1, keepdims=True)

    def one_pass(j, carry):
        t, cnt_t = carry
        cand = t + (jnp.int32(1) << (31 - j))
        cnt = count(lambda blk: blk >= cand)
        ok = cnt >= kk
        return jnp.where(ok, cand, t), jnp.where(ok, cnt, cnt_t)

    t_thr, cnt_thr = lax.fori_loop(0, 32, one_pass,
                                   (jnp.full((rows_all, 1), INT_MIN, I32), jnp.zeros((rows_all, 1), F32)))
    need = jnp.logical_and(cnt_thr > kk, t_thr > INT_MIN)

    @pl.when(jnp.max(jnp.where(need, 1.0, 0.0)) > 0.0)
    def _():
        quota = kk - count(lambda blk: blk > t_thr)
        nbits = max(1, (lp - 1).bit_length())

        def idx_pass(j, lim):
            cand = lim + (jnp.int32(1) << (nbits - 1 - j))
            cnt = count(lambda blk: jnp.logical_and(blk == t_thr, kpos_all < cand))
            return jnp.where(cnt < quota, cand, lim)

        last = lax.fori_loop(0, nbits, idx_pass, jnp.zeros((rows_all, 1), I32))
        blk = key_sc[...]
        drop = jnp.logical_and(jnp.logical_and(blk == t_thr, kpos_all > last), need)
        key_sc[...] = jnp.where(drop, t_thr - 1, blk)

    t_eff = jnp.maximum(t_thr, INT_MIN + 1)
    bias_ref[...] = jnp.where(key_sc[...] >= t_eff, 0.0, NEG)


def _idx_sample(page_table, iq, iw, ikbf, cache_ik, *, layer, group, t_new, topk):
    n = iq.shape[0]
    db, n_pages = page_table.shape
    past = n_pages * PAGE_SIZE
    lp = past + LANES
    rows = group * t_new
    blk = lambda w: pl.BlockSpec((rows, w), lambda i, pt: (i, 0))
    return pl.pallas_call(
        functools.partial(_idx_sample_kernel, group=group, n_pages=n_pages, layer=layer, t_new=t_new, topk=topk),
        grid_spec=pltpu.PrefetchScalarGridSpec(
            num_scalar_prefetch=1,
            grid=(db // group,),
            in_specs=[blk(iq.shape[1]), blk(IDX_HEADS), blk(IDX_DIM), pl.BlockSpec(memory_space=pl.ANY)],
            out_specs=blk(lp),
            scratch_shapes=[pltpu.VMEM((2, group, IDX_DIM, past), F32), pltpu.SemaphoreType.DMA((2,)),
                            pltpu.VMEM((rows, lp), I32)]),
        out_shape=jax.ShapeDtypeStruct((n, lp), F32),
        compiler_params=_params(("arbitrary",)),
        name="idx_sample",
    )(page_table, iq, iw, ikbf, cache_ik)


def _attn_sample_kernel(pt_ref, q_ref, bias_ref, knew_ref, vnew_ref, sz_ref, ck_ref, cv_ref, y_ref,
                        kbuf, vbuf, ksem, vsem, *, group, n_pages, layer, t_new):
    past = n_pages * PAGE_SIZE
    lp = past + LANES
    dh = LANES
    nkv = N_KV_HEADS
    rep = N_HEADS // nkv
    copy_args = dict(group=group, n_pages=n_pages, layer=layer, rows_per_page=PAGE_SIZE * nkv, axis=0)
    slot = _paged_pipeline([functools.partial(_page_copies, pt_ref, ck_ref, kbuf, ksem, **copy_args),
                            functools.partial(_page_copies, pt_ref, cv_ref, vbuf, vsem, **copy_args)])
    scale = dh ** -0.5
    q_all = q_ref[...].astype(F32)
    pad = jnp.zeros((LANES - t_new, dh), F32)
    for b in range(group):
        rows = slice(b * t_new, (b + 1) * t_new)
        qb = q_all[rows]
        bias = jnp.concatenate([bias_ref[rows, :]] * rep, axis=0)
        outs = []
        for g in range(nkv):
            tail = pl.ds(past * nkv + g, LANES, stride=nkv)
            kbuf[slot, b, tail, :] = jnp.concatenate([knew_ref[rows, g * dh:(g + 1) * dh], pad], axis=0)
            vbuf[slot, b, tail, :] = jnp.concatenate([vnew_ref[rows, g * dh:(g + 1) * dh], pad], axis=0)
            head = pl.ds(g, lp, stride=nkv)
            kg = _bf(kbuf[slot, b, head, :])
            vg = _bf(vbuf[slot, b, head, :])
            qg = _bf(jnp.concatenate([qb[:, (g * rep + r) * dh:(g * rep + r + 1) * dh] for r in range(rep)], axis=0))
            s = _dot_nt(qg, kg) * scale + bias
            m = jnp.max(s, axis=1, keepdims=True)
            p = jnp.exp(s - m)
            l = jnp.sum(p, axis=1, keepdims=True)
            o = _dot(_bf(p), vg) * (1.0 / l)
            outs.extend(o[r * t_new:(r + 1) * t_new] for r in range(rep))
        y_ref[rows, :] = jnp.concatenate(outs, axis=1) * sz_ref[rows, :]


def _attn_sample(page_table, q, bias, knew, vnew, sz, cache_k, cache_v, *, layer, group, t_new):
    n, bw = q.shape
    db, n_pages = page_table.shape
    past = n_pages * PAGE_SIZE
    lp = past + LANES
    kvw = knew.shape[1]
    rows = group * t_new
    blk = lambda w: pl.BlockSpec((rows, w), lambda i, pt: (i, 0))
    return pl.pallas_call(
        functools.partial(_attn_sample_kernel, group=group, n_pages=n_pages, layer=layer, t_new=t_new),
        grid_spec=pltpu.PrefetchScalarGridSpec(
            num_scalar_prefetch=1,
            grid=(db // group,),
            in_specs=[blk(bw), blk(lp), blk(kvw), blk(kvw), blk(bw),
                      pl.BlockSpec(memory_space=pl.ANY), pl.BlockSpec(memory_space=pl.ANY)],
            out_specs=blk(bw),
            scratch_shapes=[pltpu.VMEM((2, group, lp * N_KV_HEADS, LANES), F32),
                            pltpu.VMEM((2, group, lp * N_KV_HEADS, LANES), F32),
                            pltpu.SemaphoreType.DMA((2,)), pltpu.SemaphoreType.DMA((2,))]),
        out_shape=jax.ShapeDtypeStruct((n, bw), F32),
        compiler_params=_params(("arbitrary",)),
        name="attn_sample",
    )(page_table, q, bias, knew, vnew, sz, cache_k, cache_v)


def _out_ple_kernel(y_ref, x_ref, p_ref, wout_ref, gpe_ref, wg_ref, wpe_ref, xo_ref):
    x1 = x_ref[...] + _dot(_bf(y_ref[...]), wout_ref[...])
    xo_ref[...] = _per_layer_embed(x1, p_ref[...], gpe_ref[...], wg_ref[...], wpe_ref[...])


def _out_ple(y, x, p, wout, gpe, wg, wpe, *, tm):
    n, d = x.shape
    row_spec = lambda w: pl.BlockSpec((tm, w), lambda i: (i, 0))
    return pl.pallas_call(
        _out_ple_kernel,
        grid=(n // tm,),
        in_specs=[row_spec(y.shape[1]), row_spec(d), row_spec(p.shape[1]),
                  _const_spec(wout.shape), _const_spec((1, d)), _const_spec(wg.shape), _const_spec(wpe.shape)],
        out_specs=row_spec(d),
        out_shape=jax.ShapeDtypeStruct((n, d), F32),
        compiler_params=_params(("arbitrary",)),
        name="out_ple",
    )(y, x, p, wout, gpe, wg, wpe)


def _rope_tables(pos):
    def angles(half):
        inv = ROPE_THETA ** (-jnp.arange(half, dtype=F32) / half)
        return pos.astype(F32)[:, None] * inv[None, :]
    a = angles(LANES // 2)
    c128 = jnp.concatenate([jnp.cos(a), jnp.cos(a)], axis=1)
    s128 = jnp.concatenate([-jnp.sin(a), jnp.sin(a)], axis=1)
    a = angles(IDX_DIM // 2)
    zero = jnp.zeros_like(a)
    c64 = jnp.tile(jnp.cos(a), (1, 4))
    s64a = jnp.tile(jnp.concatenate([-jnp.sin(a), zero], axis=1), (1, 2))
    s64b = jnp.tile(jnp.concatenate([zero, jnp.sin(a)], axis=1), (1, 2))
    return c128, s128, c64, s64a, s64b


def _tile_rows(n, pref):
    t = min(n, pref)
    assert n % t == 0, (n, t)
    return t


def kernel(x_prompt, x_sample, cache_k, cache_v, cache_ik, page_table, p_prompt, p_sample, g_norm, w_in_a, w_s_a, b_s_a, g_v_a, w_out_a, w_in_b, g_q_b, g_k_b, g_ik_b, w_out_b, g_pe, w_pe_gate, w_pe_proj):
    bsz, s_len, d = x_prompt.shape
    db, t_new, _ = x_sample.shape
    depth = g_norm.shape[0]
    n_pool, nb_layers = cache_k.shape[0], cache_k.shape[1]
    n_pages = page_table.shape[1]
    past = n_pages * PAGE_SIZE
    aw = g_v_a.shape[1]
    bw = N_HEADS * LANES
    kvw = N_KV_HEADS * LANES
    iqw = IDX_HEADS * IDX_DIM
    assert d == bw and s_len % CHUNK == 0 and CHUNK % t_new == 0 and t_new % 8 == 0
    assert w_in_b.shape[2] == 2 * bw + 2 * kvw + iqw + IDX_DIM + IDX_HEADS

    n_p, n_s = bsz * s_len, db * t_new
    yp = x_prompt.reshape(n_p, d)
    ys = x_sample.reshape(n_s, d)
    pp = p_prompt.reshape(depth, n_p, -1)
    ps = p_sample.reshape(depth, n_s, -1)
    ck = cache_k.reshape(n_pool, nb_layers, PAGE_SIZE * N_KV_HEADS, LANES)
    cv = cache_v.reshape(n_pool, nb_layers, PAGE_SIZE * N_KV_HEADS, LANES)
    cik = jnp.swapaxes(cache_ik, 2, 3)

    tm_p = _tile_rows(n_p, 256)
    tm_s = _tile_rows(n_s, 256)
    tq = _tile_rows(s_len, 256)
    topk_p = min(TOPK_MAX, s_len // 4)
    topk_s = min(TOPK_MAX, (past + t_new) // 4)
    idx_group = _tile_rows(db, 8)
    att_group = _tile_rows(db, 2)

    tabs_p = _rope_tables(jnp.arange(s_len))
    tabs_s = _rope_tables(past + (jnp.arange(tm_s) % t_new))

    gn = g_norm.reshape(depth, 1, d)
    gpe = g_pe.reshape(depth, 1, d)
    wg = _bf(w_pe_gate)
    wpe = _bf(w_pe_proj)

    kp_l, vp_l, ikp_l, ks_l, vs_l, iks_l, cv_l = [], [], [], [], [], [], []
    for i in range(depth):
        j = i // 2
        if i % 2 == 0:
            win = _bf(w_in_a[j])
            wout = _bf(w_out_a[j])
            gv = g_v_a[j].reshape(1, aw)
            reps = CHUNK // t_new
            ws_p = w_s_a[j]
            bs_p = jnp.repeat(b_s_a[j].T, aw // A_GROUPS, axis=1)
            ws_s = jnp.tile(w_s_a[j][:, :t_new, :t_new], (1, reps, reps))
            bs_s = jnp.tile(jnp.repeat(b_s_a[j][:, :t_new].T, aw // A_GROUPS, axis=1), (reps, 1))
            yp, _ = _layer_a(yp, pp[i], gn[i], win, ws_p, bs_p, gv, wout, gpe[i], wg[i], wpe[i],
                             chunk=CHUNK, tm=tm_p, want_v=False)
            ys, v_rows = _layer_a(ys, ps[i], gn[i], win, ws_s, bs_s, gv, wout, gpe[i], wg[i], wpe[i],
                                  chunk=t_new, tm=tm_s, want_v=True)
            cv_l.append(v_rows.reshape(db, t_new, aw))
        else:
            w = w_in_b[j]
            o_iq = bw + 2 * kvw
            o_ik = o_iq + iqw
            o_iw = o_ik + IDX_DIM
            o_z = o_iw + IDX_HEADS
            wmain = _bf(jnp.concatenate([w[:, :o_ik], w[:, o_z:]], axis=1))
            wsmall = _bf(jnp.pad(w[:, o_ik:o_z], ((0, 0), (0, LANES - IDX_DIM - IDX_HEADS))))
            wiwt = _bf(w[:, o_iw:o_z].T)
            gq = g_q_b[j].reshape(1, LANES)
            gk = g_k_b[j].reshape(1, LANES)
            gik = jnp.pad(g_ik_b[j], (0, LANES - IDX_DIM)).reshape(1, LANES)
            wout = _bf(w_out_b[j])

            pr = _proj_b(yp, gn[i], wmain, wsmall, wiwt, gq, gk, gik, tabs_p, tm=tm_p, kc=tq)
            kp_l.append(pr[1].reshape(bsz, s_len, N_KV_HEADS, LANES))
            vp_l.append(pr[2].reshape(bsz, s_len, N_KV_HEADS, LANES))
            ikp_l.append(pr[3].reshape(bsz, s_len, IDX_DIM))
            yp = _dsa_prompt(pr, yp, pp[i], wout, gpe[i], wg[i], wpe[i], bsz=bsz, s_len=s_len, tq=tq, topk=topk_p)

            sr = _proj_b(ys, gn[i], wmain, wsmall, wiwt, gq, gk, gik, tabs_s, tm=tm_s, kc=min(tm_s, LANES))
            q_s, k_s, v_s, ik_s, _, _, ikbf_s, iq_s, iw_s, _, sz_s = sr
            ks_l.append(k_s.reshape(db, t_new, N_KV_HEADS, LANES))
            vs_l.append(v_s.reshape(db, t_new, N_KV_HEADS, LANES))
            iks_l.append(ik_s.reshape(db, t_new, IDX_DIM))
            bias = _idx_sample(page_table, iq_s, iw_s, ikbf_s, cik, layer=j, group=idx_group, t_new=t_new, topk=topk_s)
            y_s = _attn_sample(page_table, q_s, bias, k_s, v_s, sz_s, ck, cv, layer=j, group=att_group, t_new=t_new)
            ys = _out_ple(y_s, ys, ps[i], wout, gpe[i], wg[i], wpe[i], tm=tm_s)

    return (yp.reshape(bsz, s_len, d), ys.reshape(db, t_new, d),
            jnp.stack(kp_l, axis=1), jnp.stack(vp_l, axis=1), jnp.stack(ikp_l, axis=1),
            jnp.stack(ks_l, axis=1), jnp.stack(vs_l, axis=1), jnp.stack(iks_l, axis=1),
            jnp.stack(cv_l, axis=1))
```

```python
import functools

import jax
import jax.numpy as jnp
from jax import lax
from jax.experimental import pallas as pl
from jax.experimental.pallas import tpu as pltpu

EPS = 1e-6
ROPE_THETA = 10000.0
CHUNK = 128
A_GROUPS = 8
N_HEADS = 8
N_KV_HEADS = 2
IDX_HEADS = 8
IDX_DIM = 64
TOPK_MAX = 256
PAGE_SIZE = 128

LANES = 128
SUBLANES = 8
INT_MIN = -(2 ** 31)
F32_MAX_BITS = 0x7F7FFFFF
Q_SCALE = LANES ** -0.5 * 1.4426950408889634
NEG = -1e30
VMEM_LIMIT = 56 * 1024 * 1024

F32 = jnp.float32
BF16 = jnp.bfloat16
I32 = jnp.int32


def _bf(x):
    return x.astype(BF16)


def _dot(a, b):
    return jnp.dot(a, b, preferred_element_type=F32)


def _dot_nt(a, b):
    return lax.dot_general(a, b, (((1,), (1,)), ((), ())), preferred_element_type=F32)


def _rms(x, g):
    r = lax.rsqrt(jnp.mean(x * x, axis=-1, keepdims=True) + EPS)
    return x * r * g


def _silu(z):
    return z * jax.nn.sigmoid(z)


def _per_layer_embed(x1, p, gpe, wg, wpe):
    gate = jax.nn.sigmoid(_dot(_bf(_rms(x1, gpe)), wg))
    return x1 + gate * _dot(_bf(p), wpe)


def _key_to_f32(key):
    kc = jnp.clip(key, -F32_MAX_BITS, F32_MAX_BITS)
    return pltpu.bitcast(jnp.where(kc < 0, (-kc) | INT_MIN, kc), F32)


def _topk_threshold(count_ge, shape, kk):
    def one_pass(j, carry):
        key, cnt_key = carry
        cand = key + (jnp.int32(1) << (31 - j))
        cnt = count_ge(_key_to_f32(cand))
        ok = cnt >= kk
        return jnp.where(ok, cand, key), jnp.where(ok, cnt, cnt_key)

    key, cnt = lax.fori_loop(0, 32, one_pass, (jnp.full(shape, INT_MIN, I32), jnp.zeros(shape, F32)))
    return _key_to_f32(key), cnt


def _tie_cutoff(count_tie_before, shape, quota, nbits):
    def idx_pass(j, lim):
        cand = lim + (jnp.int32(1) << (nbits - 1 - j))
        return jnp.where(count_tie_before(cand) < quota, cand, lim)

    return lax.fori_loop(0, nbits, idx_pass, jnp.zeros(shape, I32))


def _const_spec(shape):
    nd = len(shape)
    return pl.BlockSpec(shape, lambda *_: (0,) * nd, pipeline_mode=pl.Buffered(1))


def _params(sem):
    return pltpu.CompilerParams(dimension_semantics=sem, vmem_limit_bytes=VMEM_LIMIT)


def _layer_a_kernel(x_ref, p_ref, gn_ref, win_ref, ws_ref, bs_ref, gv_ref, wout_ref,
                    gpe_ref, wg_ref, wpe_ref, xo_ref, *v_out, chunk):
    aw = gv_ref.shape[1]
    gd = aw // A_GROUPS
    x = x_ref[...]
    tm = x.shape[0]
    h = _bf(_rms(x, gn_ref[...]))
    u = _dot(h, win_ref[:, 0:aw])
    v = _rms(_dot(h, win_ref[:, aw:2 * aw]), gv_ref[...])
    z = _dot(h, win_ref[:, 2 * aw:3 * aw])
    if v_out:
        v_out[0][...] = v
    vb = _bf(v)
    row = lax.broadcasted_iota(I32, (CHUNK, CHUNK), 0)
    col = lax.broadcasted_iota(I32, (CHUNK, CHUNK), 1)
    keep = col <= row
    if chunk != CHUNK:
        keep = jnp.logical_and(keep, (row // chunk) == (col // chunk))
    cols = []
    for g in range(A_GROUPS):
        wsg = _bf(jnp.where(keep, ws_ref[g], 0.0))
        rows = [_dot(wsg, vb[j * CHUNK:(j + 1) * CHUNK, g * gd:(g + 1) * gd]) for j in range(tm // CHUNK)]
        cols.append(rows[0] if len(rows) == 1 else jnp.concatenate(rows, axis=0))
    mixed = jnp.concatenate(cols, axis=1)
    bias = bs_ref[...]
    if tm != CHUNK:
        bias = jnp.concatenate([bias] * (tm // CHUNK), axis=0)
    y = u * (mixed + bias) * _silu(z)
    x1 = x + _dot(_bf(y), wout_ref[...])
    xo_ref[...] = _per_layer_embed(x1, p_ref[...], gpe_ref[...], wg_ref[...], wpe_ref[...])


def _layer_p_spec(p, layer, rows, row_index):
    return pl.BlockSpec((None, rows, p.shape[2]), lambda *g: (layer, row_index(*g), 0))


def _layer_a(x, p, gn, win, ws_t, bs_t, gv, wout, gpe, wg, wpe, *, layer, chunk, tm, want_v):
    n, d = x.shape
    aw = gv.shape[1]
    pd = p.shape[2]
    row_spec = lambda w: pl.BlockSpec((tm, w), lambda i: (i, 0))
    out_shape = [jax.ShapeDtypeStruct((n, d), F32)]
    out_specs = [row_spec(d)]
    if want_v:
        out_shape.append(jax.ShapeDtypeStruct((n, aw), F32))
        out_specs.append(row_spec(aw))
    res = pl.pallas_call(
        functools.partial(_layer_a_kernel, chunk=chunk),
        grid=(n // tm,),
        in_specs=[row_spec(d), _layer_p_spec(p, layer, tm, lambda i: i), _const_spec((1, d)), _const_spec((d, 3 * aw)),
                  _const_spec((A_GROUPS, CHUNK, CHUNK)), _const_spec((CHUNK, aw)), _const_spec((1, aw)),
                  _const_spec((aw, d)), _const_spec((1, d)), _const_spec((d, d)), _const_spec((pd, d))],
        out_specs=out_specs,
        out_shape=out_shape,
        compiler_params=_params(("arbitrary",)),
        name="layer_a",
    )(x, p, gn, win, ws_t, bs_t, gv, wout, gpe, wg, wpe)
    return res if want_v else (res[0], None)


def _proj_b_kernel(x_ref, gn_ref, wmain_ref, wsmall_ref, wiwt_ref, gq_ref, gk_ref, gik_ref,
                   c128_ref, s128_ref, c64_ref, s64a_ref, s64b_ref,
                   q_ref, k_ref, v_ref, ik_ref, kbf_ref, vt_ref, ikbf_ref, iq_ref, iw_ref, iwt_ref, sz_ref, *, kc):
    dh = LANES
    bw = N_HEADS * dh
    kvw = N_KV_HEADS * dh
    iqw = IDX_HEADS * IDX_DIM
    x = x_ref[...]
    tm = x.shape[0]
    h = _bf(_rms(x, gn_ref[...]))
    small = _dot(h, wsmall_ref[...])
    iwt_ref[...] = _dot_nt(wiwt_ref[...], h)
    o_q, o_k, o_v, o_z = iqw, iqw + bw, iqw + bw + kvw, iqw + bw + 2 * kvw
    m_iq = _dot(h, wmain_ref[:, 0:o_q])
    m_q = _dot(h, wmain_ref[:, o_q:o_k])
    m_k = _dot(h, wmain_ref[:, o_k:o_v])
    v = _dot(h, wmain_ref[:, o_v:o_z])
    m_z = _dot(h, wmain_ref[:, o_z:o_z + bw])
    c128, s128 = c128_ref[...], s128_ref[...]
    c64, s64a, s64b = c64_ref[...], s64a_ref[...], s64b_ref[...]

    def rope128(t):
        return t * c128 + pltpu.roll(t, dh // 2, 1) * s128

    def rope64(t):
        return t * c64 + pltpu.roll(t, LANES - IDX_DIM // 2, 1) * s64a + pltpu.roll(t, IDX_DIM // 2, 1) * s64b

    lane = lax.broadcasted_iota(I32, (tm, LANES), 1)
    ms = jnp.sum(jnp.where(lane < IDX_DIM, small * small, 0.0), axis=-1, keepdims=True) * (1.0 / IDX_DIM)
    ikr = rope64(small * lax.rsqrt(ms + EPS) * gik_ref[...])[:, :IDX_DIM]
    ik_ref[...] = ikr
    ikbf_ref[...] = _bf(ikr)
    iw_ref[...] = small[:, IDX_DIM:IDX_DIM + IDX_HEADS]
    for j in range(iqw // LANES):
        iq_ref[:, j * LANES:(j + 1) * LANES] = _bf(rope64(m_iq[:, j * LANES:(j + 1) * LANES]))
    for hh in range(N_HEADS):
        qh = rope128(_rms(m_q[:, hh * dh:(hh + 1) * dh], gq_ref[...]))
        q_ref[:, hh * dh:(hh + 1) * dh] = _bf(qh * Q_SCALE)
    for hh in range(N_KV_HEADS):
        kh = rope128(_rms(m_k[:, hh * dh:(hh + 1) * dh], gk_ref[...]))
        k_ref[pl.ds(hh, tm, stride=N_KV_HEADS), :] = kh
        v_ref[pl.ds(hh, tm, stride=N_KV_HEADS), :] = v[:, hh * dh:(hh + 1) * dh]
        kbf_ref[:, hh * dh:(hh + 1) * dh] = _bf(kh)
    for j in range(tm // kc):
        vt_ref[j] = _bf(v[j * kc:(j + 1) * kc, :].T)
    sz_ref[...] = _silu(m_z)


def _proj_b(x, gn, wmain, wsmall, wiwt, gq, gk, gik, tabs, *, tm, kc):
    n, d = x.shape
    bw = N_HEADS * LANES
    kvw = N_KV_HEADS * LANES
    iqw = IDX_HEADS * IDX_DIM
    period = tabs[0].shape[0] // tm
    row_spec = lambda w: pl.BlockSpec((tm, w), lambda i: (i, 0))
    tab_spec = pl.BlockSpec((tm, LANES), lambda i: (i % period, 0))
    sds = jax.ShapeDtypeStruct
    outs = [
        (sds((n, bw), BF16), row_spec(bw)),
        (sds((n * N_KV_HEADS, LANES), F32), pl.BlockSpec((tm * N_KV_HEADS, LANES), lambda i: (i, 0))),
        (sds((n * N_KV_HEADS, LANES), F32), pl.BlockSpec((tm * N_KV_HEADS, LANES), lambda i: (i, 0))),
        (sds((n, IDX_DIM), F32), row_spec(IDX_DIM)),
        (sds((n, kvw), BF16), row_spec(kvw)),
        (sds((n // kc, kvw, kc), BF16), pl.BlockSpec((tm // kc, kvw, kc), lambda i: (i, 0, 0))),
        (sds((n, IDX_DIM), BF16), row_spec(IDX_DIM)),
        (sds((n, iqw), BF16), row_spec(iqw)),
        (sds((n, IDX_HEADS), F32), row_spec(IDX_HEADS)),
        (sds((IDX_HEADS, n), F32), pl.BlockSpec((IDX_HEADS, tm), lambda i: (0, i))),
        (sds((n, bw), F32), row_spec(bw)),
    ]
    return pl.pallas_call(
        functools.partial(_proj_b_kernel, kc=kc),
        grid=(n // tm,),
        in_specs=[row_spec(d), _const_spec((1, d)), _const_spec(wmain.shape), _const_spec(wsmall.shape),
                  _const_spec(wiwt.shape), _const_spec((1, LANES)), _const_spec((1, LANES)), _const_spec((1, LANES)),
                  tab_spec, tab_spec, tab_spec, tab_spec, tab_spec],
        out_specs=[o[1] for o in outs],
        out_shape=[o[0] for o in outs],
        compiler_params=_params(("arbitrary",)),
        name="proj_b",
    )(x, gn, wmain, wsmall, wiwt, gq, gk, gik, *tabs)


def _dsa_prompt_kernel(iq_ref, iwt_ref, ikbf_ref, kbf_ref, vt_ref, q_ref, sz_ref, x_ref, p_ref,
                       wout_ref, gpe_ref, wg_ref, wpe_ref, xo_ref, sc_ref, m_sc, l_sc, acc_sc, *, topk):
    tq = q_ref.shape[0]
    s_len = sc_ref.shape[0]
    dh = LANES
    i = pl.program_id(1)
    nch = i + 1
    qpos = i * tq + lax.broadcasted_iota(I32, (1, tq), 1)
    krow = lax.broadcasted_iota(I32, (tq, tq), 0)
    idx_scale = IDX_DIM ** -0.5 * IDX_HEADS ** -0.5

    def chunk_start(c):
        return pl.multiple_of(c * tq, tq)

    def build(c, carry):
        off = chunk_start(c)
        ikc = ikbf_ref[pl.ds(off, tq), :]
        xs = [_dot_nt(ikc, iq_ref[:, h * IDX_DIM:(h + 1) * IDX_DIM]) for h in range(IDX_HEADS)]
        s = jnp.maximum(xs[0], 0.0) * iwt_ref[0:1, :]
        for h in range(1, IDX_HEADS):
            s = s + jnp.maximum(xs[h], 0.0) * iwt_ref[h:h + 1, :]
        sc_ref[pl.ds(off, tq), :] = jnp.where(off + krow <= qpos, s * idx_scale, -jnp.inf)
        return carry

    lax.fori_loop(0, nch, build, 0)

    def count(pred):
        def body(c, acc):
            off = chunk_start(c)
            m = jnp.where(pred(sc_ref[pl.ds(off, tq), :], off + krow), 1.0, 0.0)
            part = m[0:32]
            for r in range(1, tq // 32):
                part = part + m[r * 32:(r + 1) * 32]
            return acc + part
        acc = lax.fori_loop(0, nch, body, jnp.zeros((32, tq), F32))
        return jnp.sum(acc, axis=0, keepdims=True)

    kk = float(topk)
    t_thr, cnt_thr = _topk_threshold(lambda t: count(lambda blk, pos: blk >= t), (1, tq), kk)

    need = cnt_thr > kk

    @pl.when(jnp.max(jnp.where(need, 1.0, 0.0)) > 0.0)
    def _():
        quota = kk - count(lambda blk, pos: blk > t_thr)
        last = _tie_cutoff(lambda cand: count(lambda blk, pos: jnp.logical_and(blk == t_thr, pos < cand)),
                           (1, tq), quota, max(1, (s_len - 1).bit_length()))

        def demote(c, carry):
            off = chunk_start(c)
            blk = sc_ref[pl.ds(off, tq), :]
            drop = jnp.logical_and(jnp.logical_and(blk == t_thr, off + krow > last), need)
            sc_ref[pl.ds(off, tq), :] = jnp.where(drop, -jnp.inf, blk)
            return carry

        lax.fori_loop(0, nch, demote, 0)

    m_sc[...] = jnp.full(m_sc.shape, NEG, F32)
    l_sc[...] = jnp.zeros(l_sc.shape, F32)
    acc_sc[...] = jnp.zeros(acc_sc.shape, F32)
    rep = N_HEADS // N_KV_HEADS
    sub = SUBLANES

    def attend(c, carry):
        off = chunk_start(c)
        bias = jnp.where(sc_ref[pl.ds(off, tq), :] >= t_thr, 0.0, NEG)
        kc_all = kbf_ref[pl.ds(off, tq), :]
        qk = [_dot_nt(kc_all[:, (h // rep) * dh:(h // rep + 1) * dh], q_ref[:, h * dh:(h + 1) * dh])
              for h in range(N_HEADS)]
        for g in range(N_KV_HEADS):
            vg = vt_ref[c, g * dh:(g + 1) * dh, :]
            for r in range(rep):
                h = g * rep + r
                st = slice(h * sub, (h + 1) * sub)
                s = (bias + qk[h]).reshape(tq // sub, sub, tq)
                m_old = m_sc[st, :]
                m_new = jnp.maximum(m_old, jnp.max(jnp.max(s, axis=0), axis=0, keepdims=True))
                p = jnp.exp2(s - m_new[None])
                alpha = jnp.exp2(m_old - m_new)
                l_sc[st, :] = alpha * l_sc[st, :] + jnp.sum(p, axis=0)
                acc_sc[h * dh:(h + 1) * dh, :] = (alpha[0:1] * acc_sc[h * dh:(h + 1) * dh, :]
                                                  + _dot(vg, _bf(p.reshape(tq, tq))))
                m_sc[st, :] = m_new
        return carry

    lax.fori_loop(0, nch, attend, 0)

    inv = [1.0 / jnp.sum(l_sc[h * sub:(h + 1) * sub, :], axis=0, keepdims=True) for h in range(N_HEADS)]
    att_t = jnp.concatenate([acc_sc[h * dh:(h + 1) * dh, :] * inv[h] for h in range(N_HEADS)], axis=0)
    y = _bf(att_t.T * sz_ref[...])
    x1 = x_ref[...] + _dot(y, wout_ref[...])
    xo_ref[...] = _per_layer_embed(x1, p_ref[...], gpe_ref[...], wg_ref[...], wpe_ref[...])


def _dsa_prompt(pr, x, p, wout, gpe, wg, wpe, *, layer, bsz, s_len, tq, topk):
    q, _, _, _, kbf, vt, ikbf, iq, _, iwt, sz = pr
    n, d = x.shape
    bw = q.shape[1]
    kvw = kbf.shape[1]
    nq = s_len // tq
    blk = lambda w: pl.BlockSpec((tq, w), lambda b, i: (b * nq + i, 0))
    per_b = lambda w: pl.BlockSpec((s_len, w), lambda b, i: (b, 0))
    return pl.pallas_call(
        functools.partial(_dsa_prompt_kernel, topk=topk),
        grid=(bsz, nq),
        in_specs=[blk(iq.shape[1]),
                  pl.BlockSpec((IDX_HEADS, tq), lambda b, i: (0, b * nq + i)),
                  per_b(IDX_DIM), per_b(kvw),
                  pl.BlockSpec((nq, kvw, tq), lambda b, i: (b, 0, 0)),
                  blk(bw), blk(bw), blk(d), _layer_p_spec(p, layer, tq, lambda b, i: b * nq + i),
                  _const_spec(wout.shape), _const_spec((1, d)), _const_spec(wg.shape), _const_spec(wpe.shape)],
        out_specs=blk(d),
        out_shape=jax.ShapeDtypeStruct((n, d), F32),
        scratch_shapes=[pltpu.VMEM((s_len, tq), F32), pltpu.VMEM((N_HEADS * SUBLANES, tq), F32),
                        pltpu.VMEM((N_HEADS * SUBLANES, tq), F32), pltpu.VMEM((bw, tq), F32)],
        compiler_params=_params(("arbitrary", "arbitrary")),
        name="dsa_prompt",
    )(iq, iwt, ikbf, kbf, vt, q, sz, x, p, wout, gpe, wg, wpe)


def _page_copies(pt_ref, cache_ref, buf, sem, step, slot, *, group, n_pages, layer, rows_per_page, axis):
    copies = []
    for b in range(group):
        for n in range(n_pages):
            page = pt_ref[step * group + b, n]
            span = pl.ds(n * rows_per_page, rows_per_page)
            dst = buf.at[slot, b, span, :] if axis == 0 else buf.at[slot, b, :, span]
            copies.append(pltpu.make_async_copy(cache_ref.at[page, layer], dst, sem.at[slot]))
    return copies


def _paged_pipeline(make_copies_list):
    step = pl.program_id(0)
    nsteps = pl.num_programs(0)
    slot = step % 2

    @pl.when(step == 0)
    def _():
        for mk in make_copies_list:
            for cp in mk(step, slot):
                cp.start()

    @pl.when(step + 1 < nsteps)
    def _():
        for mk in make_copies_list:
            for cp in mk(step + 1, 1 - slot):
                cp.start()

    for mk in make_copies_list:
        for cp in mk(step, slot):
            cp.wait()
    return slot


def _idx_sample_kernel(pt_ref, iq_ref, iw_ref, iknew_ref, cik_ref, bias_ref, ikbuf, sem, sc_ref,
                       *, group, n_pages, layer, t_new, topk):
    past = n_pages * PAGE_SIZE
    lp = past + LANES
    slot = _paged_pipeline([functools.partial(_page_copies, pt_ref, cik_ref, ikbuf, sem, group=group,
                                              n_pages=n_pages, layer=layer, rows_per_page=PAGE_SIZE, axis=1)])
    idx_scale = IDX_DIM ** -0.5 * IDX_HEADS ** -0.5
    iq_all = iq_ref[...].astype(F32)
    iknew_all = iknew_ref[...].astype(F32)
    iw_all = iw_ref[...]
    kpos = lax.broadcasted_iota(I32, (t_new, lp), 1)
    qpos = past + lax.broadcasted_iota(I32, (t_new, lp), 0)
    for b in range(group):
        rows = slice(b * t_new, (b + 1) * t_new)
        iqb = iq_all[rows]
        iqs = _bf(jnp.concatenate([iqb[:, h * IDX_DIM:(h + 1) * IDX_DIM] for h in range(IDX_HEADS)], axis=0))
        ik_new = _bf(jnp.concatenate([iknew_all[rows], jnp.zeros((LANES - t_new, IDX_DIM), F32)], axis=0))
        xp = _dot(iqs, _bf(ikbuf[slot, b]))
        xn = _dot_nt(iqs, ik_new)
        iwb = iw_all[rows]
        sp = jnp.zeros((t_new, past), F32)
        sn = jnp.zeros((t_new, LANES), F32)
        for h in range(IDX_HEADS):
            w = iwb[:, h:h + 1]
            sp = sp + jnp.maximum(xp[h * t_new:(h + 1) * t_new], 0.0) * w
            sn = sn + jnp.maximum(xn[h * t_new:(h + 1) * t_new], 0.0) * w
        sc = jnp.concatenate([sp, sn], axis=1) * idx_scale
        sc_ref[rows, :] = jnp.where(kpos <= qpos, sc, -jnp.inf)

    kk = float(topk)
    rows_all = group * t_new
    kpos_all = lax.broadcasted_iota(I32, (rows_all, lp), 1)

    def count(pred):
        return jnp.sum(jnp.where(pred(sc_ref[...]), 1.0, 0.0), axis=1, keepdims=True)

    t_thr, cnt_thr = _topk_threshold(lambda t: count(lambda blk: blk >= t), (rows_all, 1), kk)
    need = cnt_thr > kk

    @pl.when(jnp.max(jnp.where(need, 1.0, 0.0)) > 0.0)
    def _():
        quota = kk - count(lambda blk: blk > t_thr)
        last = _tie_cutoff(lambda cand: count(lambda blk: jnp.logical_and(blk == t_thr, kpos_all < cand)),
                           (rows_all, 1), quota, max(1, (lp - 1).bit_length()))
        blk = sc_ref[...]
        drop = jnp.logical_and(jnp.logical_and(blk == t_thr, kpos_all > last), need)
        sc_ref[...] = jnp.where(drop, -jnp.inf, blk)

    bias_ref[...] = jnp.where(sc_ref[...] >= t_thr, 0.0, NEG)


def _idx_sample(page_table, iq, iw, ikbf, cache_ik, *, layer, group, t_new, topk):
    n = iq.shape[0]
    db, n_pages = page_table.shape
    past = n_pages * PAGE_SIZE
    lp = past + LANES
    rows = group * t_new
    blk = lambda w: pl.BlockSpec((rows, w), lambda i, pt: (i, 0))
    return pl.pallas_call(
        functools.partial(_idx_sample_kernel, group=group, n_pages=n_pages, layer=layer, t_new=t_new, topk=topk),
        grid_spec=pltpu.PrefetchScalarGridSpec(
            num_scalar_prefetch=1,
            grid=(db // group,),
            in_specs=[blk(iq.shape[1]), blk(IDX_HEADS), blk(IDX_DIM), pl.BlockSpec(memory_space=pl.ANY)],
            out_specs=blk(lp),
            scratch_shapes=[pltpu.VMEM((2, group, IDX_DIM, past), F32), pltpu.SemaphoreType.DMA((2,)),
                            pltpu.VMEM((rows, lp), F32)]),
        out_shape=jax.ShapeDtypeStruct((n, lp), F32),
        compiler_params=_params(("arbitrary",)),
        name="idx_sample",
    )(page_table, iq, iw, ikbf, cache_ik)


def _attn_sample_kernel(pt_ref, q_ref, bias_ref, knew_ref, vnew_ref, sz_ref, ck_ref, cv_ref, y_ref,
                        kbuf, vbuf, ksem, vsem, *, group, n_pages, layer, t_new):
    past = n_pages * PAGE_SIZE
    lp = past + LANES
    dh = LANES
    nkv = N_KV_HEADS
    rep = N_HEADS // nkv
    copy_args = dict(group=group, n_pages=n_pages, layer=layer, rows_per_page=PAGE_SIZE * nkv, axis=0)
    slot = _paged_pipeline([functools.partial(_page_copies, pt_ref, ck_ref, kbuf, ksem, **copy_args),
                            functools.partial(_page_copies, pt_ref, cv_ref, vbuf, vsem, **copy_args)])
    q_all = q_ref[...].astype(F32)
    pad = jnp.zeros(((LANES - t_new) * nkv, dh), F32)
    for b in range(group):
        rows = slice(b * t_new, (b + 1) * t_new)
        new_rows = slice(b * t_new * nkv, (b + 1) * t_new * nkv)
        tail = pl.ds(past * nkv, LANES * nkv)
        kbuf[slot, b, tail, :] = jnp.concatenate([knew_ref[new_rows, :], pad], axis=0)
        vbuf[slot, b, tail, :] = jnp.concatenate([vnew_ref[new_rows, :], pad], axis=0)
        qb = q_all[rows]
        bias = jnp.concatenate([bias_ref[rows, :]] * rep, axis=0)
        outs = []
        for g in range(nkv):
            head = pl.ds(g, lp, stride=nkv)
            kg = _bf(kbuf[slot, b, head, :])
            vg = _bf(vbuf[slot, b, head, :])
            qg = _bf(jnp.concatenate([qb[:, (g * rep + r) * dh:(g * rep + r + 1) * dh] for r in range(rep)], axis=0))
            s = bias + _dot_nt(qg, kg)
            m = jnp.max(s, axis=1, keepdims=True)
            p = jnp.exp2(s - m)
            l = jnp.sum(p, axis=1, keepdims=True)
            o = _dot(_bf(p), vg) * (1.0 / l)
            outs.extend(o[r * t_new:(r + 1) * t_new] for r in range(rep))
        y_ref[rows, :] = jnp.concatenate(outs, axis=1) * sz_ref[rows, :]


def _attn_sample(page_table, q, bias, knew, vnew, sz, cache_k, cache_v, *, layer, group, t_new):
    n, bw = q.shape
    db, n_pages = page_table.shape
    past = n_pages * PAGE_SIZE
    lp = past + LANES
    rows = group * t_new
    blk = lambda w: pl.BlockSpec((rows, w), lambda i, pt: (i, 0))
    new_blk = pl.BlockSpec((rows * N_KV_HEADS, LANES), lambda i, pt: (i, 0))
    return pl.pallas_call(
        functools.partial(_attn_sample_kernel, group=group, n_pages=n_pages, layer=layer, t_new=t_new),
        grid_spec=pltpu.PrefetchScalarGridSpec(
            num_scalar_prefetch=1,
            grid=(db // group,),
            in_specs=[blk(bw), blk(lp), new_blk, new_blk, blk(bw),
                      pl.BlockSpec(memory_space=pl.ANY), pl.BlockSpec(memory_space=pl.ANY)],
            out_specs=blk(bw),
            scratch_shapes=[pltpu.VMEM((2, group, lp * N_KV_HEADS, LANES), F32),
                            pltpu.VMEM((2, group, lp * N_KV_HEADS, LANES), F32),
                            pltpu.SemaphoreType.DMA((2,)), pltpu.SemaphoreType.DMA((2,))]),
        out_shape=jax.ShapeDtypeStruct((n, bw), F32),
        compiler_params=_params(("arbitrary",)),
        name="attn_sample",
    )(page_table, q, bias, knew, vnew, sz, cache_k, cache_v)


def _out_ple_kernel(y_ref, x_ref, p_ref, wout_ref, gpe_ref, wg_ref, wpe_ref, xo_ref):
    x1 = x_ref[...] + _dot(_bf(y_ref[...]), wout_ref[...])
    xo_ref[...] = _per_layer_embed(x1, p_ref[...], gpe_ref[...], wg_ref[...], wpe_ref[...])


def _out_ple(y, x, p, wout, gpe, wg, wpe, *, layer, tm):
    n, d = x.shape
    row_spec = lambda w: pl.BlockSpec((tm, w), lambda i: (i, 0))
    return pl.pallas_call(
        _out_ple_kernel,
        grid=(n // tm,),
        in_specs=[row_spec(y.shape[1]), row_spec(d), _layer_p_spec(p, layer, tm, lambda i: i),
                  _const_spec(wout.shape), _const_spec((1, d)), _const_spec(wg.shape), _const_spec(wpe.shape)],
        out_specs=row_spec(d),
        out_shape=jax.ShapeDtypeStruct((n, d), F32),
        compiler_params=_params(("arbitrary",)),
        name="out_ple",
    )(y, x, p, wout, gpe, wg, wpe)


def _rope_tables(pos):
    def angles(half):
        inv = ROPE_THETA ** (-jnp.arange(half, dtype=F32) / half)
        return pos.astype(F32)[:, None] * inv[None, :]
    a = angles(LANES // 2)
    c128 = jnp.concatenate([jnp.cos(a), jnp.cos(a)], axis=1)
    s128 = jnp.concatenate([-jnp.sin(a), jnp.sin(a)], axis=1)
    a = angles(IDX_DIM // 2)
    zero = jnp.zeros_like(a)
    c64 = jnp.tile(jnp.cos(a), (1, 4))
    s64a = jnp.tile(jnp.concatenate([-jnp.sin(a), zero], axis=1), (1, 2))
    s64b = jnp.tile(jnp.concatenate([zero, jnp.sin(a)], axis=1), (1, 2))
    return c128, s128, c64, s64a, s64b


def _tile_rows(n, pref):
    t = min(n, pref)
    assert n % t == 0, (n, t)
    return t


def kernel(x_prompt, x_sample, cache_k, cache_v, cache_ik, page_table, p_prompt, p_sample, g_norm, w_in_a, w_s_a, b_s_a, g_v_a, w_out_a, w_in_b, g_q_b, g_k_b, g_ik_b, w_out_b, g_pe, w_pe_gate, w_pe_proj):
    bsz, s_len, d = x_prompt.shape
    db, t_new, _ = x_sample.shape
    depth = g_norm.shape[0]
    n_pool, nb_layers = cache_k.shape[0], cache_k.shape[1]
    n_pages = page_table.shape[1]
    past = n_pages * PAGE_SIZE
    aw = g_v_a.shape[1]
    bw = N_HEADS * LANES
    kvw = N_KV_HEADS * LANES
    iqw = IDX_HEADS * IDX_DIM
    assert d == bw and s_len % CHUNK == 0 and CHUNK % t_new == 0 and t_new % 8 == 0
    assert w_in_b.shape[2] == 2 * bw + 2 * kvw + iqw + IDX_DIM + IDX_HEADS

    n_p, n_s = bsz * s_len, db * t_new
    yp = x_prompt.reshape(n_p, d)
    ys = x_sample.reshape(n_s, d)
    pp = p_prompt.reshape(depth, n_p, -1)
    ps = p_sample.reshape(depth, n_s, -1)
    ck = cache_k.reshape(n_pool, nb_layers, PAGE_SIZE * N_KV_HEADS, LANES)
    cv = cache_v.reshape(n_pool, nb_layers, PAGE_SIZE * N_KV_HEADS, LANES)
    cik = jnp.swapaxes(cache_ik, 2, 3)

    tm_p = _tile_rows(n_p, 256)
    tm_s = _tile_rows(n_s, 256)
    tq = _tile_rows(s_len, 256)
    topk_p = min(TOPK_MAX, s_len // 4)
    topk_s = min(TOPK_MAX, (past + t_new) // 4)
    idx_group = _tile_rows(db, 16)
    att_group = _tile_rows(db, 2)

    tabs_p = _rope_tables(jnp.arange(s_len))
    tabs_s = _rope_tables(past + (jnp.arange(tm_s) % t_new))

    gn = g_norm.reshape(depth, 1, d)
    gpe = g_pe.reshape(depth, 1, d)
    wg = _bf(w_pe_gate)
    wpe = _bf(w_pe_proj)

    kp_l, vp_l, ikp_l, ks_l, vs_l, iks_l, cv_l = [], [], [], [], [], [], []
    for i in range(depth):
        j = i // 2
        if i % 2 == 0:
            win = _bf(w_in_a[j])
            wout = _bf(w_out_a[j])
            gv = g_v_a[j].reshape(1, aw)
            reps = CHUNK // t_new
            ws_p = w_s_a[j]
            bs_p = jnp.repeat(b_s_a[j].T, aw // A_GROUPS, axis=1)
            ws_s = jnp.tile(w_s_a[j][:, :t_new, :t_new], (1, reps, reps))
            bs_s = jnp.tile(jnp.repeat(b_s_a[j][:, :t_new].T, aw // A_GROUPS, axis=1), (reps, 1))
            yp, _ = _layer_a(yp, pp, gn[i], win, ws_p, bs_p, gv, wout, gpe[i], wg[i], wpe[i],
                             layer=i, chunk=CHUNK, tm=tm_p, want_v=False)
            ys, v_rows = _layer_a(ys, ps, gn[i], win, ws_s, bs_s, gv, wout, gpe[i], wg[i], wpe[i],
                                  layer=i, chunk=t_new, tm=tm_s, want_v=True)
            cv_l.append(v_rows.reshape(db, t_new, aw))
        else:
            w = w_in_b[j]
            o_iq = bw + 2 * kvw
            o_ik = o_iq + iqw
            o_iw = o_ik + IDX_DIM
            o_z = o_iw + IDX_HEADS
            wmain = _bf(jnp.concatenate([w[:, o_iq:o_ik], w[:, :o_iq], w[:, o_z:]], axis=1))
            wsmall = _bf(jnp.pad(w[:, o_ik:o_z], ((0, 0), (0, LANES - IDX_DIM - IDX_HEADS))))
            wiwt = _bf(w[:, o_iw:o_z].T)
            gq = g_q_b[j].reshape(1, LANES)
            gk = g_k_b[j].reshape(1, LANES)
            gik = jnp.pad(g_ik_b[j], (0, LANES - IDX_DIM)).reshape(1, LANES)
            wout = _bf(w_out_b[j])

            pr = _proj_b(yp, gn[i], wmain, wsmall, wiwt, gq, gk, gik, tabs_p, tm=tm_p, kc=tq)
            kp_l.append(pr[1].reshape(bsz, s_len, N_KV_HEADS, LANES))
            vp_l.append(pr[2].reshape(bsz, s_len, N_KV_HEADS, LANES))
            ikp_l.append(pr[3].reshape(bsz, s_len, IDX_DIM))
            yp = _dsa_prompt(pr, yp, pp, wout, gpe[i], wg[i], wpe[i],
                             layer=i, bsz=bsz, s_len=s_len, tq=tq, topk=topk_p)

            sr = _proj_b(ys, gn[i], wmain, wsmall, wiwt, gq, gk, gik, tabs_s, tm=tm_s, kc=min(tm_s, LANES))
            q_s, k_s, v_s, ik_s, _, _, ikbf_s, iq_s, iw_s, _, sz_s = sr
            ks_l.append(k_s.reshape(db, t_new, N_KV_HEADS, LANES))
            vs_l.append(v_s.reshape(db, t_new, N_KV_HEADS, LANES))
            iks_l.append(ik_s.reshape(db, t_new, IDX_DIM))
            bias = _idx_sample(page_table, iq_s, iw_s, ikbf_s, cik, layer=j, group=idx_group, t_new=t_new, topk=topk_s)
            y_s = _attn_sample(page_table, q_s, bias, k_s, v_s, sz_s, ck, cv, layer=j, group=att_group, t_new=t_new)
            ys = _out_ple(y_s, ys, ps, wout, gpe[i], wg[i], wpe[i], layer=i, tm=tm_s)

    return (yp.reshape(bsz, s_len, d), ys.reshape(db, t_new, d),
            jnp.stack(kp_l, axis=1), jnp.stack(vp_l, axis=1), jnp.stack(ikp_l, axis=1),
            jnp.stack(ks_l, axis=1), jnp.stack(vs_l, axis=1), jnp.stack(iks_l, axis=1),
            jnp.stack(cv_l, axis=1))
```

```python
import functools

import jax
import jax.numpy as jnp
from jax import lax
from jax.experimental import pallas as pl
from jax.experimental.pallas import tpu as pltpu

EPS = 1e-6
ROPE_THETA = 10000.0
CHUNK = 128
A_GROUPS = 8
N_HEADS = 8
N_KV_HEADS = 2
IDX_HEADS = 8
IDX_DIM = 64
TOPK_MAX = 256
PAGE_SIZE = 128

LANES = 128
SUBLANES = 8
INT_MIN = -(2 ** 31)
F32_MAX_BITS = 0x7F7FFFFF
Q_SCALE = LANES ** -0.5 * 1.4426950408889634
NEG = -1e30
VMEM_LIMIT = 56 * 1024 * 1024

F32 = jnp.float32
BF16 = jnp.bfloat16
I32 = jnp.int32


def _bf(x):
    return x.astype(BF16)


def _dot(a, b):
    return jnp.dot(a, b, preferred_element_type=F32)


def _dot_nt(a, b):
    return lax.dot_general(a, b, (((1,), (1,)), ((), ())), preferred_element_type=F32)


def _rms(x, g):
    r = lax.rsqrt(jnp.mean(x * x, axis=-1, keepdims=True) + EPS)
    return x * r * g


def _silu(z):
    return z * jax.nn.sigmoid(z)


def _per_layer_embed(x1, p, gpe, wg, wpe):
    gate = jax.nn.sigmoid(_dot(_bf(_rms(x1, gpe)), wg))
    return x1 + gate * _dot(_bf(p), wpe)


def _key_to_f32(key):
    kc = jnp.clip(key, -F32_MAX_BITS, F32_MAX_BITS)
    return pltpu.bitcast(jnp.where(kc < 0, (-kc) | INT_MIN, kc), F32)


def _topk_threshold(count_ge, shape, kk):
    def one_pass(j, carry):
        key, cnt_key = carry
        cand = key + (jnp.int32(1) << (31 - j))
        cnt = count_ge(_key_to_f32(cand))
        ok = cnt >= kk
        return jnp.where(ok, cand, key), jnp.where(ok, cnt, cnt_key)

    key, cnt = lax.fori_loop(0, 32, one_pass, (jnp.full(shape, INT_MIN, I32), jnp.zeros(shape, F32)))
    return _key_to_f32(key), cnt


def _tie_cutoff(count_tie_before, shape, quota, nbits):
    def idx_pass(j, lim):
        cand = lim + (jnp.int32(1) << (nbits - 1 - j))
        return jnp.where(count_tie_before(cand) < quota, cand, lim)

    return lax.fori_loop(0, nbits, idx_pass, jnp.zeros(shape, I32))


def _const_spec(shape):
    nd = len(shape)
    return pl.BlockSpec(shape, lambda *_: (0,) * nd, pipeline_mode=pl.Buffered(1))


def _params(sem):
    return pltpu.CompilerParams(dimension_semantics=sem, vmem_limit_bytes=VMEM_LIMIT)


def _layer_a_kernel(x_ref, p_ref, gn_ref, win_ref, ws_ref, bs_ref, gv_ref, wout_ref,
                    gpe_ref, wg_ref, wpe_ref, xo_ref, *v_out, chunk):
    aw = gv_ref.shape[1]
    gd = aw // A_GROUPS
    x = x_ref[...]
    tm = x.shape[0]
    h = _bf(_rms(x, gn_ref[...]))
    u = _dot(h, win_ref[:, 0:aw])
    v = _rms(_dot(h, win_ref[:, aw:2 * aw]), gv_ref[...])
    z = _dot(h, win_ref[:, 2 * aw:3 * aw])
    if v_out:
        v_out[0][...] = v
    vb = _bf(v)
    row = lax.broadcasted_iota(I32, (CHUNK, CHUNK), 0)
    col = lax.broadcasted_iota(I32, (CHUNK, CHUNK), 1)
    keep = col <= row
    if chunk != CHUNK:
        keep = jnp.logical_and(keep, (row // chunk) == (col // chunk))
    cols = []
    for g in range(A_GROUPS):
        wsg = _bf(jnp.where(keep, ws_ref[g], 0.0))
        rows = [_dot(wsg, vb[j * CHUNK:(j + 1) * CHUNK, g * gd:(g + 1) * gd]) for j in range(tm // CHUNK)]
        cols.append(rows[0] if len(rows) == 1 else jnp.concatenate(rows, axis=0))
    mixed = jnp.concatenate(cols, axis=1)
    bias = bs_ref[...]
    if tm != CHUNK:
        bias = jnp.concatenate([bias] * (tm // CHUNK), axis=0)
    y = u * (mixed + bias) * _silu(z)
    x1 = x + _dot(_bf(y), wout_ref[...])
    xo_ref[...] = _per_layer_embed(x1, p_ref[...], gpe_ref[...], wg_ref[...], wpe_ref[...])


def _layer_p_spec(p, layer, rows, row_index):
    return pl.BlockSpec((None, rows, p.shape[2]), lambda *g: (layer, row_index(*g), 0))


def _layer_a(x, p, gn, win, ws_t, bs_t, gv, wout, gpe, wg, wpe, *, layer, chunk, tm, want_v):
    n, d = x.shape
    aw = gv.shape[1]
    pd = p.shape[2]
    row_spec = lambda w: pl.BlockSpec((tm, w), lambda i: (i, 0))
    out_shape = [jax.ShapeDtypeStruct((n, d), F32)]
    out_specs = [row_spec(d)]
    if want_v:
        out_shape.append(jax.ShapeDtypeStruct((n, aw), F32))
        out_specs.append(row_spec(aw))
    res = pl.pallas_call(
        functools.partial(_layer_a_kernel, chunk=chunk),
        grid=(n // tm,),
        in_specs=[row_spec(d), _layer_p_spec(p, layer, tm, lambda i: i), _const_spec((1, d)), _const_spec((d, 3 * aw)),
                  _const_spec((A_GROUPS, CHUNK, CHUNK)), _const_spec((CHUNK, aw)), _const_spec((1, aw)),
                  _const_spec((aw, d)), _const_spec((1, d)), _const_spec((d, d)), _const_spec((pd, d))],
        out_specs=out_specs,
        out_shape=out_shape,
        compiler_params=_params(("arbitrary",)),
        name="layer_a",
    )(x, p, gn, win, ws_t, bs_t, gv, wout, gpe, wg, wpe)
    return res if want_v else (res[0], None)


def _proj_b_kernel(x_ref, gn_ref, wmain_ref, wsmall_ref, wiwt_ref, gq_ref, gk_ref, gik_ref,
                   c128_ref, s128_ref, c64_ref, s64a_ref, s64b_ref,
                   q_ref, k_ref, v_ref, ik_ref, kbf_ref, vt_ref, ikbf_ref, iq_ref, iw_ref, iwt_ref, sz_ref, *, kc):
    dh = LANES
    bw = N_HEADS * dh
    kvw = N_KV_HEADS * dh
    iqw = IDX_HEADS * IDX_DIM
    x = x_ref[...]
    tm = x.shape[0]
    h = _bf(_rms(x, gn_ref[...]))
    small = _dot(h, wsmall_ref[...])
    iwt_ref[...] = _dot_nt(wiwt_ref[...], h)
    o_q, o_k, o_v, o_z = iqw, iqw + bw, iqw + bw + kvw, iqw + bw + 2 * kvw
    m_iq = _dot(h, wmain_ref[:, 0:o_q])
    m_q = _dot(h, wmain_ref[:, o_q:o_k])
    m_k = _dot(h, wmain_ref[:, o_k:o_v])
    v = _dot(h, wmain_ref[:, o_v:o_z])
    m_z = _dot(h, wmain_ref[:, o_z:o_z + bw])
    c128, s128 = c128_ref[...], s128_ref[...]
    c64, s64a, s64b = c64_ref[...], s64a_ref[...], s64b_ref[...]

    def rope128(t):
        return t * c128 + pltpu.roll(t, dh // 2, 1) * s128

    def rope64(t):
        return t * c64 + pltpu.roll(t, LANES - IDX_DIM // 2, 1) * s64a + pltpu.roll(t, IDX_DIM // 2, 1) * s64b

    lane = lax.broadcasted_iota(I32, (tm, LANES), 1)
    ms = jnp.sum(jnp.where(lane < IDX_DIM, small * small, 0.0), axis=-1, keepdims=True) * (1.0 / IDX_DIM)
    ikr = rope64(small * lax.rsqrt(ms + EPS) * gik_ref[...])[:, :IDX_DIM]
    ik_ref[...] = ikr
    ikbf_ref[...] = _bf(ikr)
    iw_ref[...] = small[:, IDX_DIM:IDX_DIM + IDX_HEADS]
    for j in range(iqw // LANES):
        iq_ref[:, j * LANES:(j + 1) * LANES] = _bf(rope64(m_iq[:, j * LANES:(j + 1) * LANES]))
    for hh in range(N_HEADS):
        qh = rope128(_rms(m_q[:, hh * dh:(hh + 1) * dh], gq_ref[...]))
        q_ref[:, hh * dh:(hh + 1) * dh] = _bf(qh * Q_SCALE)
    for hh in range(N_KV_HEADS):
        kh = rope128(_rms(m_k[:, hh * dh:(hh + 1) * dh], gk_ref[...]))
        k_ref[pl.ds(hh, tm, stride=N_KV_HEADS), :] = kh
        v_ref[pl.ds(hh, tm, stride=N_KV_HEADS), :] = v[:, hh * dh:(hh + 1) * dh]
        kbf_ref[:, hh * dh:(hh + 1) * dh] = _bf(kh)
    for j in range(tm // kc):
        vt_ref[j] = _bf(v[j * kc:(j + 1) * kc, :].T)
    sz_ref[...] = _silu(m_z)


def _proj_b(x, gn, wmain, wsmall, wiwt, gq, gk, gik, tabs, *, tm, kc):
    n, d = x.shape
    bw = N_HEADS * LANES
    kvw = N_KV_HEADS * LANES
    iqw = IDX_HEADS * IDX_DIM
    period = tabs[0].shape[0] // tm
    row_spec = lambda w: pl.BlockSpec((tm, w), lambda i: (i, 0))
    tab_spec = pl.BlockSpec((tm, LANES), lambda i: (i % period, 0))
    sds = jax.ShapeDtypeStruct
    outs = [
        (sds((n, bw), BF16), row_spec(bw)),
        (sds((n * N_KV_HEADS, LANES), F32), pl.BlockSpec((tm * N_KV_HEADS, LANES), lambda i: (i, 0))),
        (sds((n * N_KV_HEADS, LANES), F32), pl.BlockSpec((tm * N_KV_HEADS, LANES), lambda i: (i, 0))),
        (sds((n, IDX_DIM), F32), row_spec(IDX_DIM)),
        (sds((n, kvw), BF16), row_spec(kvw)),
        (sds((n // kc, kvw, kc), BF16), pl.BlockSpec((tm // kc, kvw, kc), lambda i: (i, 0, 0))),
        (sds((n, IDX_DIM), BF16), row_spec(IDX_DIM)),
        (sds((n, iqw), BF16), row_spec(iqw)),
        (sds((n, IDX_HEADS), F32), row_spec(IDX_HEADS)),
        (sds((IDX_HEADS, n), F32), pl.BlockSpec((IDX_HEADS, tm), lambda i: (0, i))),
        (sds((n, bw), F32), row_spec(bw)),
    ]
    return pl.pallas_call(
        functools.partial(_proj_b_kernel, kc=kc),
        grid=(n // tm,),
        in_specs=[row_spec(d), _const_spec((1, d)), _const_spec(wmain.shape), _const_spec(wsmall.shape),
                  _const_spec(wiwt.shape), _const_spec((1, LANES)), _const_spec((1, LANES)), _const_spec((1, LANES)),
                  tab_spec, tab_spec, tab_spec, tab_spec, tab_spec],
        out_specs=[o[1] for o in outs],
        out_shape=[o[0] for o in outs],
        compiler_params=_params(("arbitrary",)),
        name="proj_b",
    )(x, gn, wmain, wsmall, wiwt, gq, gk, gik, *tabs)


def _dsa_prompt_kernel(iq_ref, iwt_ref, ikbf_ref, kbf_ref, vt_ref, q_ref, sz_ref, x_ref, p_ref,
                       wout_ref, gpe_ref, wg_ref, wpe_ref, xo_ref, sc_ref, m_sc, l_sc, acc_sc, *, topk):
    tq = q_ref.shape[0]
    s_len = sc_ref.shape[0]
    dh = LANES
    i = pl.program_id(1)
    nch = i + 1
    qpos = i * tq + lax.broadcasted_iota(I32, (1, tq), 1)
    krow = lax.broadcasted_iota(I32, (tq, tq), 0)
    idx_scale = IDX_DIM ** -0.5 * IDX_HEADS ** -0.5

    def chunk_start(c):
        return pl.multiple_of(c * tq, tq)

    def idx_dots(c):
        ikc = ikbf_ref[pl.ds(chunk_start(c), tq), :]
        return [_dot_nt(ikc, iq_ref[:, h * IDX_DIM:(h + 1) * IDX_DIM]) for h in range(IDX_HEADS)]

    def idx_combine(c, xs):
        off = chunk_start(c)
        s = jnp.maximum(xs[0], 0.0) * iwt_ref[0:1, :]
        for h in range(1, IDX_HEADS):
            s = s + jnp.maximum(xs[h], 0.0) * iwt_ref[h:h + 1, :]
        sc_ref[pl.ds(off, tq), :] = jnp.where(off + krow <= qpos, s * idx_scale, -jnp.inf)

    def build_pair(j, carry):
        xs0 = idx_dots(2 * j)
        xs1 = idx_dots(2 * j + 1)
        idx_combine(2 * j, xs0)
        idx_combine(2 * j + 1, xs1)
        return carry

    lax.fori_loop(0, nch // 2, build_pair, 0)

    @pl.when(nch % 2 == 1)
    def _():
        idx_combine(nch - 1, idx_dots(nch - 1))

    def count(pred):
        def body(c, acc):
            off = chunk_start(c)
            m = jnp.where(pred(sc_ref[pl.ds(off, tq), :], off + krow), 1.0, 0.0)
            part = m[0:32]
            for r in range(1, tq // 32):
                part = part + m[r * 32:(r + 1) * 32]
            return acc + part
        acc = lax.fori_loop(0, nch, body, jnp.zeros((32, tq), F32))
        return jnp.sum(acc, axis=0, keepdims=True)

    kk = float(topk)
    t_thr, cnt_thr = _topk_threshold(lambda t: count(lambda blk, pos: blk >= t), (1, tq), kk)

    need = cnt_thr > kk

    @pl.when(jnp.max(jnp.where(need, 1.0, 0.0)) > 0.0)
    def _():
        quota = kk - count(lambda blk, pos: blk > t_thr)
        last = _tie_cutoff(lambda cand: count(lambda blk, pos: jnp.logical_and(blk == t_thr, pos < cand)),
                           (1, tq), quota, max(1, (s_len - 1).bit_length()))

        def demote(c, carry):
            off = chunk_start(c)
            blk = sc_ref[pl.ds(off, tq), :]
            drop = jnp.logical_and(jnp.logical_and(blk == t_thr, off + krow > last), need)
            sc_ref[pl.ds(off, tq), :] = jnp.where(drop, -jnp.inf, blk)
            return carry

        lax.fori_loop(0, nch, demote, 0)

    m_sc[...] = jnp.full(m_sc.shape, NEG, F32)
    l_sc[...] = jnp.zeros(l_sc.shape, F32)
    acc_sc[...] = jnp.zeros(acc_sc.shape, F32)
    rep = N_HEADS // N_KV_HEADS
    sub = SUBLANES

    def qk_dots(c):
        kc_all = kbf_ref[pl.ds(chunk_start(c), tq), :]
        return [_dot_nt(kc_all[:, (h // rep) * dh:(h // rep + 1) * dh], q_ref[:, h * dh:(h + 1) * dh])
                for h in range(N_HEADS)]

    def softmax_pv(c, qk):
        bias = jnp.where(sc_ref[pl.ds(chunk_start(c), tq), :] >= t_thr, 0.0, NEG)
        for g in range(N_KV_HEADS):
            vg = vt_ref[c, g * dh:(g + 1) * dh, :]
            for r in range(rep):
                h = g * rep + r
                st = slice(h * sub, (h + 1) * sub)
                s = (bias + qk[h]).reshape(tq // sub, sub, tq)
                m_old = m_sc[st, :]
                m_new = jnp.maximum(m_old, jnp.max(jnp.max(s, axis=0), axis=0, keepdims=True))
                p = jnp.exp2(s - m_new[None])
                alpha = jnp.exp2(m_old - m_new)
                l_sc[st, :] = alpha * l_sc[st, :] + jnp.sum(p, axis=0)
                acc_sc[h * dh:(h + 1) * dh, :] = (alpha[0:1] * acc_sc[h * dh:(h + 1) * dh, :]
                                                  + _dot(vg, _bf(p.reshape(tq, tq))))
                m_sc[st, :] = m_new

    def attend_pair(j, carry):
        qk0 = qk_dots(2 * j)
        qk1 = qk_dots(2 * j + 1)
        softmax_pv(2 * j, qk0)
        softmax_pv(2 * j + 1, qk1)
        return carry

    lax.fori_loop(0, nch // 2, attend_pair, 0)

    @pl.when(nch % 2 == 1)
    def _():
        softmax_pv(nch - 1, qk_dots(nch - 1))

    inv = [1.0 / jnp.sum(l_sc[h * sub:(h + 1) * sub, :], axis=0, keepdims=True) for h in range(N_HEADS)]
    att_t = jnp.concatenate([acc_sc[h * dh:(h + 1) * dh, :] * inv[h] for h in range(N_HEADS)], axis=0)
    y = _bf(att_t.T * sz_ref[...])
    x1 = x_ref[...] + _dot(y, wout_ref[...])
    xo_ref[...] = _per_layer_embed(x1, p_ref[...], gpe_ref[...], wg_ref[...], wpe_ref[...])


def _dsa_prompt(pr, x, p, wout, gpe, wg, wpe, *, layer, bsz, s_len, tq, topk):
    q, _, _, _, kbf, vt, ikbf, iq, _, iwt, sz = pr
    n, d = x.shape
    bw = q.shape[1]
    kvw = kbf.shape[1]
    nq = s_len // tq
    blk = lambda w: pl.BlockSpec((tq, w), lambda b, i: (b * nq + i, 0))
    per_b = lambda w: pl.BlockSpec((s_len, w), lambda b, i: (b, 0))
    return pl.pallas_call(
        functools.partial(_dsa_prompt_kernel, topk=topk),
        grid=(bsz, nq),
        in_specs=[blk(iq.shape[1]),
                  pl.BlockSpec((IDX_HEADS, tq), lambda b, i: (0, b * nq + i)),
                  per_b(IDX_DIM), per_b(kvw),
                  pl.BlockSpec((nq, kvw, tq), lambda b, i: (b, 0, 0)),
                  blk(bw), blk(bw), blk(d), _layer_p_spec(p, layer, tq, lambda b, i: b * nq + i),
                  _const_spec(wout.shape), _const_spec((1, d)), _const_spec(wg.shape), _const_spec(wpe.shape)],
        out_specs=blk(d),
        out_shape=jax.ShapeDtypeStruct((n, d), F32),
        scratch_shapes=[pltpu.VMEM((s_len, tq), F32), pltpu.VMEM((N_HEADS * SUBLANES, tq), F32),
                        pltpu.VMEM((N_HEADS * SUBLANES, tq), F32), pltpu.VMEM((bw, tq), F32)],
        compiler_params=_params(("arbitrary", "arbitrary")),
        name="dsa_prompt",
    )(iq, iwt, ikbf, kbf, vt, q, sz, x, p, wout, gpe, wg, wpe)


def _page_copies(pt_ref, cache_ref, buf, sem, step, slot, *, group, n_pages, layer, rows_per_page, axis):
    copies = []
    for b in range(group):
        for n in range(n_pages):
            page = pt_ref[step * group + b, n]
            span = pl.ds(n * rows_per_page, rows_per_page)
            dst = buf.at[slot, b, span, :] if axis == 0 else buf.at[slot, b, :, span]
            copies.append(pltpu.make_async_copy(cache_ref.at[page, layer], dst, sem.at[slot]))
    return copies


def _paged_pipeline(make_copies_list):
    step = pl.program_id(0)
    nsteps = pl.num_programs(0)
    slot = step % 2

    @pl.when(step == 0)
    def _():
        for mk in make_copies_list:
            for cp in mk(step, slot):
                cp.start()

    @pl.when(step + 1 < nsteps)
    def _():
        for mk in make_copies_list:
            for cp in mk(step + 1, 1 - slot):
                cp.start()

    for mk in make_copies_list:
        for cp in mk(step, slot):
            cp.wait()
    return slot


def _idx_sample_kernel(pt_ref, iq_ref, iw_ref, iknew_ref, cik_ref, bias_ref, ikbuf, sem, sc_ref,
                       *, group, n_pages, layer, t_new, topk):
    past = n_pages * PAGE_SIZE
    lp = past + LANES
    slot = _paged_pipeline([functools.partial(_page_copies, pt_ref, cik_ref, ikbuf, sem, group=group,
                                              n_pages=n_pages, layer=layer, rows_per_page=PAGE_SIZE, axis=1)])
    idx_scale = IDX_DIM ** -0.5 * IDX_HEADS ** -0.5
    iq_all = iq_ref[...].astype(F32)
    iknew_all = iknew_ref[...].astype(F32)
    iw_all = iw_ref[...]
    kpos = lax.broadcasted_iota(I32, (t_new, lp), 1)
    qpos = past + lax.broadcasted_iota(I32, (t_new, lp), 0)
    dots = []
    for b in range(group):
        rows = slice(b * t_new, (b + 1) * t_new)
        iqb = iq_all[rows]
        iqs = _bf(jnp.concatenate([iqb[:, h * IDX_DIM:(h + 1) * IDX_DIM] for h in range(IDX_HEADS)], axis=0))
        ik_new = _bf(jnp.concatenate([iknew_all[rows], jnp.zeros((LANES - t_new, IDX_DIM), F32)], axis=0))
        dots.append((_dot(iqs, _bf(ikbuf[slot, b])), _dot_nt(iqs, ik_new)))
    for b in range(group):
        rows = slice(b * t_new, (b + 1) * t_new)
        xp, xn = dots[b]
        iwb = iw_all[rows]
        sp = jnp.zeros((t_new, past), F32)
        sn = jnp.zeros((t_new, LANES), F32)
        for h in range(IDX_HEADS):
            w = iwb[:, h:h + 1]
            sp = sp + jnp.maximum(xp[h * t_new:(h + 1) * t_new], 0.0) * w
            sn = sn + jnp.maximum(xn[h * t_new:(h + 1) * t_new], 0.0) * w
        sc = jnp.concatenate([sp, sn], axis=1) * idx_scale
        sc_ref[rows, :] = jnp.where(kpos <= qpos, sc, -jnp.inf)

    kk = float(topk)
    rows_all = group * t_new
    kpos_all = lax.broadcasted_iota(I32, (rows_all, lp), 1)

    def count(pred):
        return jnp.sum(jnp.where(pred(sc_ref[...]), 1.0, 0.0), axis=1, keepdims=True)

    t_thr, cnt_thr = _topk_threshold(lambda t: count(lambda blk: blk >= t), (rows_all, 1), kk)
    need = cnt_thr > kk

    @pl.when(jnp.max(jnp.where(need, 1.0, 0.0)) > 0.0)
    def _():
        quota = kk - count(lambda blk: blk > t_thr)
        last = _tie_cutoff(lambda cand: count(lambda blk: jnp.logical_and(blk == t_thr, kpos_all < cand)),
                           (rows_all, 1), quota, max(1, (lp - 1).bit_length()))
        blk = sc_ref[...]
        drop = jnp.logical_and(jnp.logical_and(blk == t_thr, kpos_all > last), need)
        sc_ref[...] = jnp.where(drop, -jnp.inf, blk)

    bias_ref[...] = jnp.where(sc_ref[...] >= t_thr, 0.0, NEG)


def _idx_sample(page_table, iq, iw, ikbf, cache_ik, *, layer, group, t_new, topk):
    n = iq.shape[0]
    db, n_pages = page_table.shape
    past = n_pages * PAGE_SIZE
    lp = past + LANES
    rows = group * t_new
    blk = lambda w: pl.BlockSpec((rows, w), lambda i, pt: (i, 0))
    return pl.pallas_call(
        functools.partial(_idx_sample_kernel, group=group, n_pages=n_pages, layer=layer, t_new=t_new, topk=topk),
        grid_spec=pltpu.PrefetchScalarGridSpec(
            num_scalar_prefetch=1,
            grid=(db // group,),
            in_specs=[blk(iq.shape[1]), blk(IDX_HEADS), blk(IDX_DIM), pl.BlockSpec(memory_space=pl.ANY)],
            out_specs=blk(lp),
            scratch_shapes=[pltpu.VMEM((2, group, IDX_DIM, past), F32), pltpu.SemaphoreType.DMA((2,)),
                            pltpu.VMEM((rows, lp), F32)]),
        out_shape=jax.ShapeDtypeStruct((n, lp), F32),
        compiler_params=_params(("arbitrary",)),
        name="idx_sample",
    )(page_table, iq, iw, ikbf, cache_ik)


def _attn_sample_kernel(pt_ref, q_ref, bias_ref, knew_ref, vnew_ref, sz_ref, ck_ref, cv_ref, y_ref,
                        kbuf, vbuf, ksem, vsem, *, group, n_pages, layer, t_new):
    past = n_pages * PAGE_SIZE
    lp = past + LANES
    dh = LANES
    nkv = N_KV_HEADS
    rep = N_HEADS // nkv
    copy_args = dict(group=group, n_pages=n_pages, layer=layer, rows_per_page=PAGE_SIZE * nkv, axis=0)
    slot = _paged_pipeline([functools.partial(_page_copies, pt_ref, ck_ref, kbuf, ksem, **copy_args),
                            functools.partial(_page_copies, pt_ref, cv_ref, vbuf, vsem, **copy_args)])
    q_all = q_ref[...].astype(F32)
    pad = jnp.zeros(((LANES - t_new) * nkv, dh), F32)
    qk = {}
    for b in range(group):
        rows = slice(b * t_new, (b + 1) * t_new)
        new_rows = slice(b * t_new * nkv, (b + 1) * t_new * nkv)
        tail = pl.ds(past * nkv, LANES * nkv)
        kbuf[slot, b, tail, :] = jnp.concatenate([knew_ref[new_rows, :], pad], axis=0)
        vbuf[slot, b, tail, :] = jnp.concatenate([vnew_ref[new_rows, :], pad], axis=0)
        qb = q_all[rows]
        for g in range(nkv):
            kg = _bf(kbuf[slot, b, pl.ds(g, lp, stride=nkv), :])
            qg = _bf(jnp.concatenate([qb[:, (g * rep + r) * dh:(g * rep + r + 1) * dh] for r in range(rep)], axis=0))
            qk[b, g] = _dot_nt(qg, kg)
    for b in range(group):
        rows = slice(b * t_new, (b + 1) * t_new)
        bias = jnp.concatenate([bias_ref[rows, :]] * rep, axis=0)
        outs = []
        for g in range(nkv):
            vg = _bf(vbuf[slot, b, pl.ds(g, lp, stride=nkv), :])
            s = bias + qk[b, g]
            m = jnp.max(s, axis=1, keepdims=True)
            p = jnp.exp2(s - m)
            l = jnp.sum(p, axis=1, keepdims=True)
            o = _dot(_bf(p), vg) * (1.0 / l)
            outs.extend(o[r * t_new:(r + 1) * t_new] for r in range(rep))
        y_ref[rows, :] = jnp.concatenate(outs, axis=1) * sz_ref[rows, :]


def _attn_sample(page_table, q, bias, knew, vnew, sz, cache_k, cache_v, *, layer, group, t_new):
    n, bw = q.shape
    db, n_pages = page_table.shape
    past = n_pages * PAGE_SIZE
    lp = past + LANES
    rows = group * t_new
    blk = lambda w: pl.BlockSpec((rows, w), lambda i, pt: (i, 0))
    new_blk = pl.BlockSpec((rows * N_KV_HEADS, LANES), lambda i, pt: (i, 0))
    return pl.pallas_call(
        functools.partial(_attn_sample_kernel, group=group, n_pages=n_pages, layer=layer, t_new=t_new),
        grid_spec=pltpu.PrefetchScalarGridSpec(
            num_scalar_prefetch=1,
            grid=(db // group,),
            in_specs=[blk(bw), blk(lp), new_blk, new_blk, blk(bw),
                      pl.BlockSpec(memory_space=pl.ANY), pl.BlockSpec(memory_space=pl.ANY)],
            out_specs=blk(bw),
            scratch_shapes=[pltpu.VMEM((2, group, lp * N_KV_HEADS, LANES), F32),
                            pltpu.VMEM((2, group, lp * N_KV_HEADS, LANES), F32),
                            pltpu.SemaphoreType.DMA((2,)), pltpu.SemaphoreType.DMA((2,))]),
        out_shape=jax.ShapeDtypeStruct((n, bw), F32),
        compiler_params=_params(("arbitrary",)),
        name="attn_sample",
    )(page_table, q, bias, knew, vnew, sz, cache_k, cache_v)


def _out_ple_kernel(y_ref, x_ref, p_ref, wout_ref, gpe_ref, wg_ref, wpe_ref, xo_ref):
    x1 = x_ref[...] + _dot(_bf(y_ref[...]), wout_ref[...])
    xo_ref[...] = _per_layer_embed(x1, p_ref[...], gpe_ref[...], wg_ref[...], wpe_ref[...])


def _out_ple(y, x, p, wout, gpe, wg, wpe, *, layer, tm):
    n, d = x.shape
    row_spec = lambda w: pl.BlockSpec((tm, w), lambda i: (i, 0))
    return pl.pallas_call(
        _out_ple_kernel,
        grid=(n // tm,),
        in_specs=[row_spec(y.shape[1]), row_spec(d), _layer_p_spec(p, layer, tm, lambda i: i),
                  _const_spec(wout.shape), _const_spec((1, d)), _const_spec(wg.shape), _const_spec(wpe.shape)],
        out_specs=row_spec(d),
        out_shape=jax.ShapeDtypeStruct((n, d), F32),
        compiler_params=_params(("arbitrary",)),
        name="out_ple",
    )(y, x, p, wout, gpe, wg, wpe)


def _rope_tables(pos):
    def angles(half):
        inv = ROPE_THETA ** (-jnp.arange(half, dtype=F32) / half)
        return pos.astype(F32)[:, None] * inv[None, :]
    a = angles(LANES // 2)
    c128 = jnp.concatenate([jnp.cos(a), jnp.cos(a)], axis=1)
    s128 = jnp.concatenate([-jnp.sin(a), jnp.sin(a)], axis=1)
    a = angles(IDX_DIM // 2)
    zero = jnp.zeros_like(a)
    c64 = jnp.tile(jnp.cos(a), (1, 4))
    s64a = jnp.tile(jnp.concatenate([-jnp.sin(a), zero], axis=1), (1, 2))
    s64b = jnp.tile(jnp.concatenate([zero, jnp.sin(a)], axis=1), (1, 2))
    return c128, s128, c64, s64a, s64b


def _tile_rows(n, pref):
    t = min(n, pref)
    assert n % t == 0, (n, t)
    return t


def kernel(x_prompt, x_sample, cache_k, cache_v, cache_ik, page_table, p_prompt, p_sample, g_norm, w_in_a, w_s_a, b_s_a, g_v_a, w_out_a, w_in_b, g_q_b, g_k_b, g_ik_b, w_out_b, g_pe, w_pe_gate, w_pe_proj):
    bsz, s_len, d = x_prompt.shape
    db, t_new, _ = x_sample.shape
    depth = g_norm.shape[0]
    n_pool, nb_layers = cache_k.shape[0], cache_k.shape[1]
    n_pages = page_table.shape[1]
    past = n_pages * PAGE_SIZE
    aw = g_v_a.shape[1]
    bw = N_HEADS * LANES
    kvw = N_KV_HEADS * LANES
    iqw = IDX_HEADS * IDX_DIM
    assert d == bw and s_len % CHUNK == 0 and CHUNK % t_new == 0 and t_new % 8 == 0
    assert w_in_b.shape[2] == 2 * bw + 2 * kvw + iqw + IDX_DIM + IDX_HEADS

    n_p, n_s = bsz * s_len, db * t_new
    yp = x_prompt.reshape(n_p, d)
    ys = x_sample.reshape(n_s, d)
    pp = p_prompt.reshape(depth, n_p, -1)
    ps = p_sample.reshape(depth, n_s, -1)
    ck = cache_k.reshape(n_pool, nb_layers, PAGE_SIZE * N_KV_HEADS, LANES)
    cv = cache_v.reshape(n_pool, nb_layers, PAGE_SIZE * N_KV_HEADS, LANES)
    cik = jnp.swapaxes(cache_ik, 2, 3)

    tm_p = _tile_rows(n_p, 256)
    tm_s = _tile_rows(n_s, 256)
    tq = _tile_rows(s_len, 256)
    topk_p = min(TOPK_MAX, s_len // 4)
    topk_s = min(TOPK_MAX, (past + t_new) // 4)
    idx_group = _tile_rows(db, 16)
    att_group = _tile_rows(db, 2)

    tabs_p = _rope_tables(jnp.arange(s_len))
    tabs_s = _rope_tables(past + (jnp.arange(tm_s) % t_new))

    gn = g_norm.reshape(depth, 1, d)
    gpe = g_pe.reshape(depth, 1, d)
    wg = _bf(w_pe_gate)
    wpe = _bf(w_pe_proj)

    kp_l, vp_l, ikp_l, ks_l, vs_l, iks_l, cv_l = [], [], [], [], [], [], []
    for i in range(depth):
        j = i // 2
        if i % 2 == 0:
            win = _bf(w_in_a[j])
            wout = _bf(w_out_a[j])
            gv = g_v_a[j].reshape(1, aw)
            reps = CHUNK // t_new
            ws_p = w_s_a[j]
            bs_p = jnp.repeat(b_s_a[j].T, aw // A_GROUPS, axis=1)
            ws_s = jnp.tile(w_s_a[j][:, :t_new, :t_new], (1, reps, reps))
            bs_s = jnp.tile(jnp.repeat(b_s_a[j][:, :t_new].T, aw // A_GROUPS, axis=1), (reps, 1))
            yp, _ = _layer_a(yp, pp, gn[i], win, ws_p, bs_p, gv, wout, gpe[i], wg[i], wpe[i],
                             layer=i, chunk=CHUNK, tm=tm_p, want_v=False)
            ys, v_rows = _layer_a(ys, ps, gn[i], win, ws_s, bs_s, gv, wout, gpe[i], wg[i], wpe[i],
                                  layer=i, chunk=t_new, tm=tm_s, want_v=True)
            cv_l.append(v_rows.reshape(db, t_new, aw))
        else:
            w = w_in_b[j]
            o_iq = bw + 2 * kvw
            o_ik = o_iq + iqw
            o_iw = o_ik + IDX_DIM
            o_z = o_iw + IDX_HEADS
            wmain = _bf(jnp.concatenate([w[:, o_iq:o_ik], w[:, :o_iq], w[:, o_z:]], axis=1))
            wsmall = _bf(jnp.pad(w[:, o_ik:o_z], ((0, 0), (0, LANES - IDX_DIM - IDX_HEADS))))
            wiwt = _bf(w[:, o_iw:o_z].T)
            gq = g_q_b[j].reshape(1, LANES)
            gk = g_k_b[j].reshape(1, LANES)
            gik = jnp.pad(g_ik_b[j], (0, LANES - IDX_DIM)).reshape(1, LANES)
            wout = _bf(w_out_b[j])

            pr = _proj_b(yp, gn[i], wmain, wsmall, wiwt, gq, gk, gik, tabs_p, tm=tm_p, kc=tq)
            kp_l.append(pr[1].reshape(bsz, s_len, N_KV_HEADS, LANES))
            vp_l.append(pr[2].reshape(bsz, s_len, N_KV_HEADS, LANES))
            ikp_l.append(pr[3].reshape(bsz, s_len, IDX_DIM))
            yp = _dsa_prompt(pr, yp, pp, wout, gpe[i], wg[i], wpe[i],
                             layer=i, bsz=bsz, s_len=s_len, tq=tq, topk=topk_p)

            sr = _proj_b(ys, gn[i], wmain, wsmall, wiwt, gq, gk, gik, tabs_s, tm=tm_s, kc=min(tm_s, LANES))
            q_s, k_s, v_s, ik_s, _, _, ikbf_s, iq_s, iw_s, _, sz_s = sr
            ks_l.append(k_s.reshape(db, t_new, N_KV_HEADS, LANES))
            vs_l.append(v_s.reshape(db, t_new, N_KV_HEADS, LANES))
            iks_l.append(ik_s.reshape(db, t_new, IDX_DIM))
            bias = _idx_sample(page_table, iq_s, iw_s, ikbf_s, cik, layer=j, group=idx_group, t_new=t_new, topk=topk_s)
            y_s = _attn_sample(page_table, q_s, bias, k_s, v_s, sz_s, ck, cv, layer=j, group=att_group, t_new=t_new)
            ys = _out_ple(y_s, ys, ps, wout, gpe[i], wg[i], wpe[i], layer=i, tm=tm_s)

    return (yp.reshape(bsz, s_len, d), ys.reshape(db, t_new, d),
            jnp.stack(kp_l, axis=1), jnp.stack(vp_l, axis=1), jnp.stack(ikp_l, axis=1),
            jnp.stack(ks_l, axis=1), jnp.stack(vs_l, axis=1), jnp.stack(iks_l, axis=1),
            jnp.stack(cv_l, axis=1))
```

```python
import functools

import jax
import jax.numpy as jnp
from jax import lax
from jax.experimental import pallas as pl
from jax.experimental.pallas import tpu as pltpu

EPS = 1e-6
ROPE_THETA = 10000.0
CHUNK = 128
A_GROUPS = 8
N_HEADS = 8
N_KV_HEADS = 2
IDX_HEADS = 8
IDX_DIM = 64
TOPK_MAX = 256
PAGE_SIZE = 128

LANES = 128
SUBLANES = 8
INT_MIN = -(2 ** 31)
F32_MAX_BITS = 0x7F7FFFFF
Q_SCALE = LANES ** -0.5 * 1.4426950408889634
NEG = -1e30
VMEM_LIMIT = 56 * 1024 * 1024

F32 = jnp.float32
BF16 = jnp.bfloat16
I32 = jnp.int32


def _bf(x):
    return x.astype(BF16)


def _dot(a, b):
    return jnp.dot(a, b, preferred_element_type=F32)


def _dot_nt(a, b):
    return lax.dot_general(a, b, (((1,), (1,)), ((), ())), preferred_element_type=F32)


def _rms(x, g):
    r = lax.rsqrt(jnp.mean(x * x, axis=-1, keepdims=True) + EPS)
    return x * r * g


def _silu(z):
    return z * jax.nn.sigmoid(z)


def _per_layer_embed(x1, p, gpe, wg, wpe):
    gate = jax.nn.sigmoid(_dot(_bf(_rms(x1, gpe)), wg))
    return x1 + gate * _dot(_bf(p), wpe)


def _key_to_f32(key):
    kc = jnp.clip(key, -F32_MAX_BITS, F32_MAX_BITS)
    return pltpu.bitcast(jnp.where(kc < 0, (-kc) | INT_MIN, kc), F32)


def _topk_threshold(count_ge, shape, kk):
    def one_pass(j, carry):
        key, cnt_key = carry
        cand = key + (jnp.int32(1) << (31 - j))
        cnt = count_ge(_key_to_f32(cand))
        ok = cnt >= kk
        return jnp.where(ok, cand, key), jnp.where(ok, cnt, cnt_key)

    key, cnt = lax.fori_loop(0, 32, one_pass, (jnp.full(shape, INT_MIN, I32), jnp.zeros(shape, F32)))
    return _key_to_f32(key), cnt


def _tie_cutoff(count_tie_before, shape, quota, nbits):
    def idx_pass(j, lim):
        cand = lim + (jnp.int32(1) << (nbits - 1 - j))
        return jnp.where(count_tie_before(cand) < quota, cand, lim)

    return lax.fori_loop(0, nbits, idx_pass, jnp.zeros(shape, I32))


def _const_spec(shape):
    nd = len(shape)
    return pl.BlockSpec(shape, lambda *_: (0,) * nd, pipeline_mode=pl.Buffered(1))


def _params(sem):
    return pltpu.CompilerParams(dimension_semantics=sem, vmem_limit_bytes=VMEM_LIMIT)


def _layer_a_kernel(x_ref, p_ref, gn_ref, win_ref, ws_ref, bs_ref, gv_ref, wout_ref,
                    gpe_ref, wg_ref, wpe_ref, xo_ref, *v_out, chunk):
    aw = gv_ref.shape[1]
    gd = aw // A_GROUPS
    x = x_ref[...]
    tm = x.shape[0]
    h = _bf(_rms(x, gn_ref[...]))
    u = _dot(h, win_ref[:, 0:aw])
    v = _rms(_dot(h, win_ref[:, aw:2 * aw]), gv_ref[...])
    z = _dot(h, win_ref[:, 2 * aw:3 * aw])
    if v_out:
        v_out[0][...] = v
    vb = _bf(v)
    row = lax.broadcasted_iota(I32, (CHUNK, CHUNK), 0)
    col = lax.broadcasted_iota(I32, (CHUNK, CHUNK), 1)
    keep = col <= row
    if chunk != CHUNK:
        keep = jnp.logical_and(keep, (row // chunk) == (col // chunk))
    cols = []
    for g in range(A_GROUPS):
        wsg = _bf(jnp.where(keep, ws_ref[g], 0.0))
        rows = [_dot(wsg, vb[j * CHUNK:(j + 1) * CHUNK, g * gd:(g + 1) * gd]) for j in range(tm // CHUNK)]
        cols.append(rows[0] if len(rows) == 1 else jnp.concatenate(rows, axis=0))
    mixed = jnp.concatenate(cols, axis=1)
    bias = bs_ref[...]
    if tm != CHUNK:
        bias = jnp.concatenate([bias] * (tm // CHUNK), axis=0)
    y = u * (mixed + bias) * _silu(z)
    x1 = x + _dot(_bf(y), wout_ref[...])
    xo_ref[...] = _per_layer_embed(x1, p_ref[...], gpe_ref[...], wg_ref[...], wpe_ref[...])


def _layer_p_spec(p, layer, rows, row_index):
    return pl.BlockSpec((None, rows, p.shape[2]), lambda *g: (layer, row_index(*g), 0))


def _layer_a(x, p, gn, win, ws_t, bs_t, gv, wout, gpe, wg, wpe, *, layer, chunk, tm, want_v):
    n, d = x.shape
    aw = gv.shape[1]
    pd = p.shape[2]
    row_spec = lambda w: pl.BlockSpec((tm, w), lambda i: (i, 0))
    out_shape = [jax.ShapeDtypeStruct((n, d), F32)]
    out_specs = [row_spec(d)]
    if want_v:
        out_shape.append(jax.ShapeDtypeStruct((n, aw), F32))
        out_specs.append(row_spec(aw))
    res = pl.pallas_call(
        functools.partial(_layer_a_kernel, chunk=chunk),
        grid=(n // tm,),
        in_specs=[row_spec(d), _layer_p_spec(p, layer, tm, lambda i: i), _const_spec((1, d)), _const_spec((d, 3 * aw)),
                  _const_spec((A_GROUPS, CHUNK, CHUNK)), _const_spec((CHUNK, aw)), _const_spec((1, aw)),
                  _const_spec((aw, d)), _const_spec((1, d)), _const_spec((d, d)), _const_spec((pd, d))],
        out_specs=out_specs,
        out_shape=out_shape,
        compiler_params=_params(("arbitrary",)),
        name="layer_a",
    )(x, p, gn, win, ws_t, bs_t, gv, wout, gpe, wg, wpe)
    return res if want_v else (res[0], None)


def _proj_b_kernel(x_ref, gn_ref, wmain_ref, wsmall_ref, wiwt_ref, gq_ref, gk_ref, gik_ref,
                   c128_ref, s128_ref, c64_ref, s64a_ref, s64b_ref,
                   q_ref, k_ref, v_ref, ik_ref, kbf_ref, vt_ref, ikbf_ref, iq_ref, iw_ref, iwt_ref, sz_ref, *, kc):
    dh = LANES
    bw = N_HEADS * dh
    kvw = N_KV_HEADS * dh
    iqw = IDX_HEADS * IDX_DIM
    x = x_ref[...]
    tm = x.shape[0]
    h = _bf(_rms(x, gn_ref[...]))
    small = _dot(h, wsmall_ref[...])
    iwt_ref[...] = _dot_nt(wiwt_ref[...], h)
    o_q, o_k, o_v, o_z = iqw, iqw + bw, iqw + bw + kvw, iqw + bw + 2 * kvw
    m_iq = _dot(h, wmain_ref[:, 0:o_q])
    m_q = _dot(h, wmain_ref[:, o_q:o_k])
    m_k = _dot(h, wmain_ref[:, o_k:o_v])
    v = _dot(h, wmain_ref[:, o_v:o_z])
    m_z = _dot(h, wmain_ref[:, o_z:o_z + bw])
    c128, s128 = c128_ref[...], s128_ref[...]
    c64, s64a, s64b = c64_ref[...], s64a_ref[...], s64b_ref[...]

    def rope128(t):
        return t * c128 + pltpu.roll(t, dh // 2, 1) * s128

    def rope64(t):
        return t * c64 + pltpu.roll(t, LANES - IDX_DIM // 2, 1) * s64a + pltpu.roll(t, IDX_DIM // 2, 1) * s64b

    lane = lax.broadcasted_iota(I32, (tm, LANES), 1)
    ms = jnp.sum(jnp.where(lane < IDX_DIM, small * small, 0.0), axis=-1, keepdims=True) * (1.0 / IDX_DIM)
    ikr = rope64(small * lax.rsqrt(ms + EPS) * gik_ref[...])[:, :IDX_DIM]
    ik_ref[...] = ikr
    ikbf_ref[...] = _bf(ikr)
    iw_ref[...] = small[:, IDX_DIM:IDX_DIM + IDX_HEADS]
    for j in range(iqw // LANES):
        iq_ref[:, j * LANES:(j + 1) * LANES] = _bf(rope64(m_iq[:, j * LANES:(j + 1) * LANES]))
    for hh in range(N_HEADS):
        qh = rope128(_rms(m_q[:, hh * dh:(hh + 1) * dh], gq_ref[...]))
        q_ref[:, hh * dh:(hh + 1) * dh] = _bf(qh * Q_SCALE)
    for hh in range(N_KV_HEADS):
        kh = rope128(_rms(m_k[:, hh * dh:(hh + 1) * dh], gk_ref[...]))
        k_ref[pl.ds(hh, tm, stride=N_KV_HEADS), :] = kh
        v_ref[pl.ds(hh, tm, stride=N_KV_HEADS), :] = v[:, hh * dh:(hh + 1) * dh]
        kbf_ref[:, hh * dh:(hh + 1) * dh] = _bf(kh)
    for j in range(tm // kc):
        vt_ref[j] = _bf(v[j * kc:(j + 1) * kc, :].T)
    sz_ref[...] = _silu(m_z)


def _proj_b(x, gn, wmain, wsmall, wiwt, gq, gk, gik, tabs, *, tm, kc):
    n, d = x.shape
    bw = N_HEADS * LANES
    kvw = N_KV_HEADS * LANES
    iqw = IDX_HEADS * IDX_DIM
    period = tabs[0].shape[0] // tm
    row_spec = lambda w: pl.BlockSpec((tm, w), lambda i: (i, 0))
    tab_spec = pl.BlockSpec((tm, LANES), lambda i: (i % period, 0))
    sds = jax.ShapeDtypeStruct
    outs = [
        (sds((n, bw), BF16), row_spec(bw)),
        (sds((n * N_KV_HEADS, LANES), F32), pl.BlockSpec((tm * N_KV_HEADS, LANES), lambda i: (i, 0))),
        (sds((n * N_KV_HEADS, LANES), F32), pl.BlockSpec((tm * N_KV_HEADS, LANES), lambda i: (i, 0))),
        (sds((n, IDX_DIM), F32), row_spec(IDX_DIM)),
        (sds((n, kvw), BF16), row_spec(kvw)),
        (sds((n // kc, kvw, kc), BF16), pl.BlockSpec((tm // kc, kvw, kc), lambda i: (i, 0, 0))),
        (sds((n, IDX_DIM), BF16), row_spec(IDX_DIM)),
        (sds((n, iqw), BF16), row_spec(iqw)),
        (sds((n, IDX_HEADS), F32), row_spec(IDX_HEADS)),
        (sds((IDX_HEADS, n), F32), pl.BlockSpec((IDX_HEADS, tm), lambda i: (0, i))),
        (sds((n, bw), F32), row_spec(bw)),
    ]
    return pl.pallas_call(
        functools.partial(_proj_b_kernel, kc=kc),
        grid=(n // tm,),
        in_specs=[row_spec(d), _const_spec((1, d)), _const_spec(wmain.shape), _const_spec(wsmall.shape),
                  _const_spec(wiwt.shape), _const_spec((1, LANES)), _const_spec((1, LANES)), _const_spec((1, LANES)),
                  tab_spec, tab_spec, tab_spec, tab_spec, tab_spec],
        out_specs=[o[1] for o in outs],
        out_shape=[o[0] for o in outs],
        compiler_params=_params(("arbitrary",)),
        name="proj_b",
    )(x, gn, wmain, wsmall, wiwt, gq, gk, gik, *tabs)


def _dsa_prompt_sample_kernel(pt_ref, *refs, topk, sample):
    prompt_in, sample_in = refs[:13], refs[13:20]
    xo_ref, y_ref = refs[20:22]
    prompt_scratch, sample_scratch = refs[22:26], refs[26:30]
    step = pl.program_id(0) * pl.num_programs(1) + pl.program_id(1)
    nsteps = pl.num_programs(0) * pl.num_programs(1)
    _sample_attention_step(pt_ref, *sample_in, y_ref, *sample_scratch, step, nsteps, **sample)
    _dsa_prompt_kernel(*prompt_in, xo_ref, *prompt_scratch, topk=topk)


def _dsa_prompt_kernel(iq_ref, iwt_ref, ikbf_ref, kbf_ref, vt_ref, q_ref, sz_ref, x_ref, p_ref,
                       wout_ref, gpe_ref, wg_ref, wpe_ref, xo_ref, sc_ref, m_sc, l_sc, acc_sc, *, topk):
    tq = q_ref.shape[0]
    s_len = sc_ref.shape[0]
    dh = LANES
    i = pl.program_id(1)
    nch = i + 1
    qpos = i * tq + lax.broadcasted_iota(I32, (1, tq), 1)
    krow = lax.broadcasted_iota(I32, (tq, tq), 0)
    idx_scale = IDX_DIM ** -0.5 * IDX_HEADS ** -0.5

    def chunk_start(c):
        return pl.multiple_of(c * tq, tq)

    def idx_dots(c):
        ikc = ikbf_ref[pl.ds(chunk_start(c), tq), :]
        return [_dot_nt(ikc, iq_ref[:, h * IDX_DIM:(h + 1) * IDX_DIM]) for h in range(IDX_HEADS)]

    def idx_combine(c, xs):
        off = chunk_start(c)
        s = jnp.maximum(xs[0], 0.0) * iwt_ref[0:1, :]
        for h in range(1, IDX_HEADS):
            s = s + jnp.maximum(xs[h], 0.0) * iwt_ref[h:h + 1, :]
        sc_ref[pl.ds(off, tq), :] = jnp.where(off + krow <= qpos, s * idx_scale, -jnp.inf)

    def build_pair(j, carry):
        xs0 = idx_dots(2 * j)
        xs1 = idx_dots(2 * j + 1)
        idx_combine(2 * j, xs0)
        idx_combine(2 * j + 1, xs1)
        return carry

    lax.fori_loop(0, nch // 2, build_pair, 0)

    @pl.when(nch % 2 == 1)
    def _():
        idx_combine(nch - 1, idx_dots(nch - 1))

    def count(pred):
        def body(c, acc):
            off = chunk_start(c)
            m = jnp.where(pred(sc_ref[pl.ds(off, tq), :], off + krow), 1.0, 0.0)
            part = m[0:32]
            for r in range(1, tq // 32):
                part = part + m[r * 32:(r + 1) * 32]
            return acc + part
        acc = lax.fori_loop(0, nch, body, jnp.zeros((32, tq), F32))
        return jnp.sum(acc, axis=0, keepdims=True)

    kk = float(topk)
    t_thr, cnt_thr = _topk_threshold(lambda t: count(lambda blk, pos: blk >= t), (1, tq), kk)

    need = cnt_thr > kk

    @pl.when(jnp.max(jnp.where(need, 1.0, 0.0)) > 0.0)
    def _():
        quota = kk - count(lambda blk, pos: blk > t_thr)
        last = _tie_cutoff(lambda cand: count(lambda blk, pos: jnp.logical_and(blk == t_thr, pos < cand)),
                           (1, tq), quota, max(1, (s_len - 1).bit_length()))

        def demote(c, carry):
            off = chunk_start(c)
            blk = sc_ref[pl.ds(off, tq), :]
            drop = jnp.logical_and(jnp.logical_and(blk == t_thr, off + krow > last), need)
            sc_ref[pl.ds(off, tq), :] = jnp.where(drop, -jnp.inf, blk)
            return carry

        lax.fori_loop(0, nch, demote, 0)

    m_sc[...] = jnp.full(m_sc.shape, NEG, F32)
    l_sc[...] = jnp.zeros(l_sc.shape, F32)
    acc_sc[...] = jnp.zeros(acc_sc.shape, F32)
    rep = N_HEADS // N_KV_HEADS
    sub = SUBLANES

    def qk_dots(c):
        kc_all = kbf_ref[pl.ds(chunk_start(c), tq), :]
        return [_dot_nt(kc_all[:, (h // rep) * dh:(h // rep + 1) * dh], q_ref[:, h * dh:(h + 1) * dh])
                for h in range(N_HEADS)]

    def softmax_pv(c, qk):
        bias = jnp.where(sc_ref[pl.ds(chunk_start(c), tq), :] >= t_thr, 0.0, NEG)
        for g in range(N_KV_HEADS):
            vg = vt_ref[c, g * dh:(g + 1) * dh, :]
            for r in range(rep):
                h = g * rep + r
                st = slice(h * sub, (h + 1) * sub)
                s = (bias + qk[h]).reshape(tq // sub, sub, tq)
                m_old = m_sc[st, :]
                m_new = jnp.maximum(m_old, jnp.max(jnp.max(s, axis=0), axis=0, keepdims=True))
                p = jnp.exp2(s - m_new[None])
                alpha = jnp.exp2(m_old - m_new)
                l_sc[st, :] = alpha * l_sc[st, :] + jnp.sum(p, axis=0)
                acc_sc[h * dh:(h + 1) * dh, :] = (alpha[0:1] * acc_sc[h * dh:(h + 1) * dh, :]
                                                  + _dot(vg, _bf(p.reshape(tq, tq))))
                m_sc[st, :] = m_new

    def attend_pair(j, carry):
        qk0 = qk_dots(2 * j)
        qk1 = qk_dots(2 * j + 1)
        softmax_pv(2 * j, qk0)
        softmax_pv(2 * j + 1, qk1)
        return carry

    lax.fori_loop(0, nch // 2, attend_pair, 0)

    @pl.when(nch % 2 == 1)
    def _():
        softmax_pv(nch - 1, qk_dots(nch - 1))

    inv = [1.0 / jnp.sum(l_sc[h * sub:(h + 1) * sub, :], axis=0, keepdims=True) for h in range(N_HEADS)]
    att_t = jnp.concatenate([acc_sc[h * dh:(h + 1) * dh, :] * inv[h] for h in range(N_HEADS)], axis=0)
    y = _bf(att_t.T * sz_ref[...])
    x1 = x_ref[...] + _dot(y, wout_ref[...])
    xo_ref[...] = _per_layer_embed(x1, p_ref[...], gpe_ref[...], wg_ref[...], wpe_ref[...])


def _dsa_prompt(pr, x, p, wout, gpe, wg, wpe, *, layer, bsz, s_len, tq, topk, sample=None):
    q, _, _, _, kbf, vt, ikbf, iq, _, iwt, sz = pr
    n, d = x.shape
    bw = q.shape[1]
    kvw = kbf.shape[1]
    nq = s_len // tq
    blk = lambda w: pl.BlockSpec((tq, w), lambda b, i, *_: (b * nq + i, 0))
    per_b = lambda w: pl.BlockSpec((s_len, w), lambda b, i, *_: (b, 0))
    in_specs = [blk(iq.shape[1]),
                pl.BlockSpec((IDX_HEADS, tq), lambda b, i, *_: (0, b * nq + i)),
                per_b(IDX_DIM), per_b(kvw),
                pl.BlockSpec((nq, kvw, tq), lambda b, i, *_: (b, 0, 0)),
                blk(bw), blk(bw), blk(d), _layer_p_spec(p, layer, tq, lambda b, i, *_: b * nq + i),
                _const_spec(wout.shape), _const_spec((1, d)), _const_spec(wg.shape), _const_spec(wpe.shape)]
    scratch = [pltpu.VMEM((s_len, tq), F32), pltpu.VMEM((N_HEADS * SUBLANES, tq), F32),
               pltpu.VMEM((N_HEADS * SUBLANES, tq), F32), pltpu.VMEM((bw, tq), F32)]
    args = (iq, iwt, ikbf, kbf, vt, q, sz, x, p, wout, gpe, wg, wpe)
    if sample is None:
        return pl.pallas_call(
            functools.partial(_dsa_prompt_kernel, topk=topk),
            grid=(bsz, nq), in_specs=in_specs, out_specs=blk(d),
            out_shape=jax.ShapeDtypeStruct((n, d), F32), scratch_shapes=scratch,
            compiler_params=_params(("arbitrary", "arbitrary")), name="dsa_prompt",
        )(*args)

    page_table, q_s, bias, k_new, v_new, sz_s, cache_k, cache_v, cache_layer, group, t_new = sample
    n_pages = page_table.shape[1]
    lp = n_pages * PAGE_SIZE + LANES
    rows = group * t_new
    sblk = lambda r, w: pl.BlockSpec((r, w), lambda b, i, *_: (b * nq + i, 0))
    any_spec = pl.BlockSpec(memory_space=pl.ANY)
    return pl.pallas_call(
        functools.partial(_dsa_prompt_sample_kernel, topk=topk,
                          sample=dict(group=group, n_pages=n_pages, layer=cache_layer, t_new=t_new)),
        grid_spec=pltpu.PrefetchScalarGridSpec(
            num_scalar_prefetch=1,
            grid=(bsz, nq),
            in_specs=in_specs + [sblk(rows, bw), sblk(rows, lp), sblk(rows * N_KV_HEADS, LANES),
                                 sblk(rows * N_KV_HEADS, LANES), sblk(rows, bw), any_spec, any_spec],
            out_specs=[blk(d), sblk(rows, bw)],
            scratch_shapes=scratch + [pltpu.VMEM((2, group, lp * N_KV_HEADS, LANES), F32),
                                      pltpu.VMEM((2, group, lp * N_KV_HEADS, LANES), F32),
                                      pltpu.SemaphoreType.DMA((2,)), pltpu.SemaphoreType.DMA((2,))]),
        out_shape=[jax.ShapeDtypeStruct((n, d), F32), jax.ShapeDtypeStruct(q_s.shape, F32)],
        compiler_params=_params(("arbitrary", "arbitrary")),
        name="dsa_prompt_sample",
    )(page_table, *args, q_s, bias, k_new, v_new, sz_s, cache_k, cache_v)


def _page_copies(pt_ref, cache_ref, buf, sem, step, slot, *, group, n_pages, layer, rows_per_page, axis):
    copies = []
    for b in range(group):
        for n in range(n_pages):
            page = pt_ref[step * group + b, n]
            span = pl.ds(n * rows_per_page, rows_per_page)
            dst = buf.at[slot, b, span, :] if axis == 0 else buf.at[slot, b, :, span]
            copies.append(pltpu.make_async_copy(cache_ref.at[page, layer], dst, sem.at[slot]))
    return copies


def _paged_pipeline(make_copies_list, step, nsteps):
    slot = step % 2

    @pl.when(step == 0)
    def _():
        for mk in make_copies_list:
            for cp in mk(step, slot):
                cp.start()

    @pl.when(step + 1 < nsteps)
    def _():
        for mk in make_copies_list:
            for cp in mk(step + 1, 1 - slot):
                cp.start()

    for mk in make_copies_list:
        for cp in mk(step, slot):
            cp.wait()
    return slot


def _idx_sample_kernel(pt_ref, iq_ref, iw_ref, iknew_ref, cik_ref, bias_ref, ikbuf, sem, sc_ref,
                       *, group, n_pages, layer, t_new, topk):
    past = n_pages * PAGE_SIZE
    lp = past + LANES
    slot = _paged_pipeline([functools.partial(_page_copies, pt_ref, cik_ref, ikbuf, sem, group=group,
                                              n_pages=n_pages, layer=layer, rows_per_page=PAGE_SIZE, axis=1)],
                           pl.program_id(0), pl.num_programs(0))
    idx_scale = IDX_DIM ** -0.5 * IDX_HEADS ** -0.5
    iq_all = iq_ref[...].astype(F32)
    iknew_all = iknew_ref[...].astype(F32)
    iw_all = iw_ref[...]
    kpos = lax.broadcasted_iota(I32, (t_new, lp), 1)
    qpos = past + lax.broadcasted_iota(I32, (t_new, lp), 0)
    dots = []
    for b in range(group):
        rows = slice(b * t_new, (b + 1) * t_new)
        iqb = iq_all[rows]
        iqs = _bf(jnp.concatenate([iqb[:, h * IDX_DIM:(h + 1) * IDX_DIM] for h in range(IDX_HEADS)], axis=0))
        ik_new = _bf(jnp.concatenate([iknew_all[rows], jnp.zeros((LANES - t_new, IDX_DIM), F32)], axis=0))
        dots.append((_dot(iqs, _bf(ikbuf[slot, b])), _dot_nt(iqs, ik_new)))
    for b in range(group):
        rows = slice(b * t_new, (b + 1) * t_new)
        xp, xn = dots[b]
        iwb = iw_all[rows]
        sp = jnp.zeros((t_new, past), F32)
        sn = jnp.zeros((t_new, LANES), F32)
        for h in range(IDX_HEADS):
            w = iwb[:, h:h + 1]
            sp = sp + jnp.maximum(xp[h * t_new:(h + 1) * t_new], 0.0) * w
            sn = sn + jnp.maximum(xn[h * t_new:(h + 1) * t_new], 0.0) * w
        sc = jnp.concatenate([sp, sn], axis=1) * idx_scale
        sc_ref[rows, :] = jnp.where(kpos <= qpos, sc, -jnp.inf)

    kk = float(topk)
    rows_all = group * t_new
    kpos_all = lax.broadcasted_iota(I32, (rows_all, lp), 1)

    def count(pred):
        return jnp.sum(jnp.where(pred(sc_ref[...]), 1.0, 0.0), axis=1, keepdims=True)

    t_thr, cnt_thr = _topk_threshold(lambda t: count(lambda blk: blk >= t), (rows_all, 1), kk)
    need = cnt_thr > kk

    @pl.when(jnp.max(jnp.where(need, 1.0, 0.0)) > 0.0)
    def _():
        quota = kk - count(lambda blk: blk > t_thr)
        last = _tie_cutoff(lambda cand: count(lambda blk: jnp.logical_and(blk == t_thr, kpos_all < cand)),
                           (rows_all, 1), quota, max(1, (lp - 1).bit_length()))
        blk = sc_ref[...]
        drop = jnp.logical_and(jnp.logical_and(blk == t_thr, kpos_all > last), need)
        sc_ref[...] = jnp.where(drop, -jnp.inf, blk)

    bias_ref[...] = jnp.where(sc_ref[...] >= t_thr, 0.0, NEG)


def _idx_sample(page_table, iq, iw, ikbf, cache_ik, *, layer, group, t_new, topk):
    n = iq.shape[0]
    db, n_pages = page_table.shape
    past = n_pages * PAGE_SIZE
    lp = past + LANES
    rows = group * t_new
    blk = lambda w: pl.BlockSpec((rows, w), lambda i, pt: (i, 0))
    return pl.pallas_call(
        functools.partial(_idx_sample_kernel, group=group, n_pages=n_pages, layer=layer, t_new=t_new, topk=topk),
        grid_spec=pltpu.PrefetchScalarGridSpec(
            num_scalar_prefetch=1,
            grid=(db // group,),
            in_specs=[blk(iq.shape[1]), blk(IDX_HEADS), blk(IDX_DIM), pl.BlockSpec(memory_space=pl.ANY)],
            out_specs=blk(lp),
            scratch_shapes=[pltpu.VMEM((2, group, IDX_DIM, past), F32), pltpu.SemaphoreType.DMA((2,)),
                            pltpu.VMEM((rows, lp), F32)]),
        out_shape=jax.ShapeDtypeStruct((n, lp), F32),
        compiler_params=_params(("arbitrary",)),
        name="idx_sample",
    )(page_table, iq, iw, ikbf, cache_ik)


def _attn_sample_kernel(pt_ref, q_ref, bias_ref, knew_ref, vnew_ref, sz_ref, ck_ref, cv_ref, y_ref,
                        kbuf, vbuf, ksem, vsem, **static):
    _sample_attention_step(pt_ref, q_ref, bias_ref, knew_ref, vnew_ref, sz_ref, ck_ref, cv_ref, y_ref,
                           kbuf, vbuf, ksem, vsem, pl.program_id(0), pl.num_programs(0), **static)


def _sample_attention_step(pt_ref, q_ref, bias_ref, knew_ref, vnew_ref, sz_ref, ck_ref, cv_ref, y_ref,
                           kbuf, vbuf, ksem, vsem, step, nsteps, *, group, n_pages, layer, t_new):
    past = n_pages * PAGE_SIZE
    lp = past + LANES
    dh = LANES
    nkv = N_KV_HEADS
    rep = N_HEADS // nkv
    copy_args = dict(group=group, n_pages=n_pages, layer=layer, rows_per_page=PAGE_SIZE * nkv, axis=0)
    slot = _paged_pipeline([functools.partial(_page_copies, pt_ref, ck_ref, kbuf, ksem, **copy_args),
                            functools.partial(_page_copies, pt_ref, cv_ref, vbuf, vsem, **copy_args)],
                           step, nsteps)
    q_all = q_ref[...].astype(F32)
    pad = jnp.zeros(((LANES - t_new) * nkv, dh), F32)
    qk = {}
    for b in range(group):
        rows = slice(b * t_new, (b + 1) * t_new)
        new_rows = slice(b * t_new * nkv, (b + 1) * t_new * nkv)
        tail = pl.ds(past * nkv, LANES * nkv)
        kbuf[slot, b, tail, :] = jnp.concatenate([knew_ref[new_rows, :], pad], axis=0)
        vbuf[slot, b, tail, :] = jnp.concatenate([vnew_ref[new_rows, :], pad], axis=0)
        qb = q_all[rows]
        for g in range(nkv):
            kg = _bf(kbuf[slot, b, pl.ds(g, lp, stride=nkv), :])
            qg = _bf(jnp.concatenate([qb[:, (g * rep + r) * dh:(g * rep + r + 1) * dh] for r in range(rep)], axis=0))
            qk[b, g] = _dot_nt(qg, kg)
    for b in range(group):
        rows = slice(b * t_new, (b + 1) * t_new)
        bias = jnp.concatenate([bias_ref[rows, :]] * rep, axis=0)
        outs = []
        for g in range(nkv):
            vg = _bf(vbuf[slot, b, pl.ds(g, lp, stride=nkv), :])
            s = bias + qk[b, g]
            m = jnp.max(s, axis=1, keepdims=True)
            p = jnp.exp2(s - m)
            l = jnp.sum(p, axis=1, keepdims=True)
            o = _dot(_bf(p), vg) * (1.0 / l)
            outs.extend(o[r * t_new:(r + 1) * t_new] for r in range(rep))
        y_ref[rows, :] = jnp.concatenate(outs, axis=1) * sz_ref[rows, :]


def _attn_sample(page_table, q, bias, knew, vnew, sz, cache_k, cache_v, *, layer, group, t_new):
    n, bw = q.shape
    db, n_pages = page_table.shape
    past = n_pages * PAGE_SIZE
    lp = past + LANES
    rows = group * t_new
    blk = lambda w: pl.BlockSpec((rows, w), lambda i, pt: (i, 0))
    new_blk = pl.BlockSpec((rows * N_KV_HEADS, LANES), lambda i, pt: (i, 0))
    return pl.pallas_call(
        functools.partial(_attn_sample_kernel, group=group, n_pages=n_pages, layer=layer, t_new=t_new),
        grid_spec=pltpu.PrefetchScalarGridSpec(
            num_scalar_prefetch=1,
            grid=(db // group,),
            in_specs=[blk(bw), blk(lp), new_blk, new_blk, blk(bw),
                      pl.BlockSpec(memory_space=pl.ANY), pl.BlockSpec(memory_space=pl.ANY)],
            out_specs=blk(bw),
            scratch_shapes=[pltpu.VMEM((2, group, lp * N_KV_HEADS, LANES), F32),
                            pltpu.VMEM((2, group, lp * N_KV_HEADS, LANES), F32),
                            pltpu.SemaphoreType.DMA((2,)), pltpu.SemaphoreType.DMA((2,))]),
        out_shape=jax.ShapeDtypeStruct((n, bw), F32),
        compiler_params=_params(("arbitrary",)),
        name="attn_sample",
    )(page_table, q, bias, knew, vnew, sz, cache_k, cache_v)


def _out_ple_kernel(y_ref, x_ref, p_ref, wout_ref, gpe_ref, wg_ref, wpe_ref, xo_ref):
    x1 = x_ref[...] + _dot(_bf(y_ref[...]), wout_ref[...])
    xo_ref[...] = _per_layer_embed(x1, p_ref[...], gpe_ref[...], wg_ref[...], wpe_ref[...])


def _out_ple(y, x, p, wout, gpe, wg, wpe, *, layer, tm):
    n, d = x.shape
    row_spec = lambda w: pl.BlockSpec((tm, w), lambda i: (i, 0))
    return pl.pallas_call(
        _out_ple_kernel,
        grid=(n // tm,),
        in_specs=[row_spec(y.shape[1]), row_spec(d), _layer_p_spec(p, layer, tm, lambda i: i),
                  _const_spec(wout.shape), _const_spec((1, d)), _const_spec(wg.shape), _const_spec(wpe.shape)],
        out_specs=row_spec(d),
        out_shape=jax.ShapeDtypeStruct((n, d), F32),
        compiler_params=_params(("arbitrary",)),
        name="out_ple",
    )(y, x, p, wout, gpe, wg, wpe)


def _rope_tables(pos):
    def angles(half):
        inv = ROPE_THETA ** (-jnp.arange(half, dtype=F32) / half)
        return pos.astype(F32)[:, None] * inv[None, :]
    a = angles(LANES // 2)
    c128 = jnp.concatenate([jnp.cos(a), jnp.cos(a)], axis=1)
    s128 = jnp.concatenate([-jnp.sin(a), jnp.sin(a)], axis=1)
    a = angles(IDX_DIM // 2)
    zero = jnp.zeros_like(a)
    c64 = jnp.tile(jnp.cos(a), (1, 4))
    s64a = jnp.tile(jnp.concatenate([-jnp.sin(a), zero], axis=1), (1, 2))
    s64b = jnp.tile(jnp.concatenate([zero, jnp.sin(a)], axis=1), (1, 2))
    return c128, s128, c64, s64a, s64b


def _tile_rows(n, pref):
    t = min(n, pref)
    assert n % t == 0, (n, t)
    return t


def kernel(x_prompt, x_sample, cache_k, cache_v, cache_ik, page_table, p_prompt, p_sample, g_norm, w_in_a, w_s_a, b_s_a, g_v_a, w_out_a, w_in_b, g_q_b, g_k_b, g_ik_b, w_out_b, g_pe, w_pe_gate, w_pe_proj):
    bsz, s_len, d = x_prompt.shape
    db, t_new, _ = x_sample.shape
    depth = g_norm.shape[0]
    n_pool, nb_layers = cache_k.shape[0], cache_k.shape[1]
    n_pages = page_table.shape[1]
    past = n_pages * PAGE_SIZE
    aw = g_v_a.shape[1]
    bw = N_HEADS * LANES
    kvw = N_KV_HEADS * LANES
    iqw = IDX_HEADS * IDX_DIM
    assert d == bw and s_len % CHUNK == 0 and CHUNK % t_new == 0 and t_new % 8 == 0
    assert w_in_b.shape[2] == 2 * bw + 2 * kvw + iqw + IDX_DIM + IDX_HEADS

    n_p, n_s = bsz * s_len, db * t_new
    yp = x_prompt.reshape(n_p, d)
    ys = x_sample.reshape(n_s, d)
    pp = p_prompt.reshape(depth, n_p, -1)
    ps = p_sample.reshape(depth, n_s, -1)
    ck = cache_k.reshape(n_pool, nb_layers, PAGE_SIZE * N_KV_HEADS, LANES)
    cv = cache_v.reshape(n_pool, nb_layers, PAGE_SIZE * N_KV_HEADS, LANES)
    cik = jnp.swapaxes(cache_ik, 2, 3)

    tm_p = _tile_rows(n_p, 256)
    tm_s = _tile_rows(n_s, 256)
    tq = _tile_rows(s_len, 256)
    topk_p = min(TOPK_MAX, s_len // 4)
    topk_s = min(TOPK_MAX, (past + t_new) // 4)
    idx_group = _tile_rows(db, 16)
    att_group = _tile_rows(db, 2)

    tabs_p = _rope_tables(jnp.arange(s_len))
    tabs_s = _rope_tables(past + (jnp.arange(tm_s) % t_new))

    gn = g_norm.reshape(depth, 1, d)
    gpe = g_pe.reshape(depth, 1, d)
    wg = _bf(w_pe_gate)
    wpe = _bf(w_pe_proj)

    kp_l, vp_l, ikp_l, ks_l, vs_l, iks_l, cv_l = [], [], [], [], [], [], []
    for i in range(depth):
        j = i // 2
        if i % 2 == 0:
            win = _bf(w_in_a[j])
            wout = _bf(w_out_a[j])
            gv = g_v_a[j].reshape(1, aw)
            reps = CHUNK // t_new
            ws_p = w_s_a[j]
            bs_p = jnp.repeat(b_s_a[j].T, aw // A_GROUPS, axis=1)
            ws_s = jnp.tile(w_s_a[j][:, :t_new, :t_new], (1, reps, reps))
            bs_s = jnp.tile(jnp.repeat(b_s_a[j][:, :t_new].T, aw // A_GROUPS, axis=1), (reps, 1))
            yp, _ = _layer_a(yp, pp, gn[i], win, ws_p, bs_p, gv, wout, gpe[i], wg[i], wpe[i],
                             layer=i, chunk=CHUNK, tm=tm_p, want_v=False)
            ys, v_rows = _layer_a(ys, ps, gn[i], win, ws_s, bs_s, gv, wout, gpe[i], wg[i], wpe[i],
                                  layer=i, chunk=t_new, tm=tm_s, want_v=True)
            cv_l.append(v_rows.reshape(db, t_new, aw))
        else:
            w = w_in_b[j]
            o_iq = bw + 2 * kvw
            o_ik = o_iq + iqw
            o_iw = o_ik + IDX_DIM
            o_z = o_iw + IDX_HEADS
            wmain = _bf(jnp.concatenate([w[:, o_iq:o_ik], w[:, :o_iq], w[:, o_z:]], axis=1))
            wsmall = _bf(jnp.pad(w[:, o_ik:o_z], ((0, 0), (0, LANES - IDX_DIM - IDX_HEADS))))
            wiwt = _bf(w[:, o_iw:o_z].T)
            gq = g_q_b[j].reshape(1, LANES)
            gk = g_k_b[j].reshape(1, LANES)
            gik = jnp.pad(g_ik_b[j], (0, LANES - IDX_DIM)).reshape(1, LANES)
            wout = _bf(w_out_b[j])

            pr = _proj_b(yp, gn[i], wmain, wsmall, wiwt, gq, gk, gik, tabs_p, tm=tm_p, kc=tq)
            kp_l.append(pr[1].reshape(bsz, s_len, N_KV_HEADS, LANES))
            vp_l.append(pr[2].reshape(bsz, s_len, N_KV_HEADS, LANES))
            ikp_l.append(pr[3].reshape(bsz, s_len, IDX_DIM))
            sr = _proj_b(ys, gn[i], wmain, wsmall, wiwt, gq, gk, gik, tabs_s, tm=tm_s, kc=min(tm_s, LANES))
            q_s, k_s, v_s, ik_s, _, _, ikbf_s, iq_s, iw_s, _, sz_s = sr
            ks_l.append(k_s.reshape(db, t_new, N_KV_HEADS, LANES))
            vs_l.append(v_s.reshape(db, t_new, N_KV_HEADS, LANES))
            iks_l.append(ik_s.reshape(db, t_new, IDX_DIM))
            bias = _idx_sample(page_table, iq_s, iw_s, ikbf_s, cik, layer=j, group=idx_group, t_new=t_new, topk=topk_s)

            dsa_args = dict(layer=i, bsz=bsz, s_len=s_len, tq=tq, topk=topk_p)
            prompt_steps = bsz * (s_len // tq)
            if db % prompt_steps == 0:
                yp, y_s = _dsa_prompt(pr, yp, pp, wout, gpe[i], wg[i], wpe[i], **dsa_args,
                                      sample=(page_table, q_s, bias, k_s, v_s, sz_s, ck, cv, j,
                                              db // prompt_steps, t_new))
            else:
                yp = _dsa_prompt(pr, yp, pp, wout, gpe[i], wg[i], wpe[i], **dsa_args)
                y_s = _attn_sample(page_table, q_s, bias, k_s, v_s, sz_s, ck, cv, layer=j, group=att_group, t_new=t_new)
            ys = _out_ple(y_s, ys, ps, wout, gpe[i], wg[i], wpe[i], layer=i, tm=tm_s)

    return (yp.reshape(bsz, s_len, d), ys.reshape(db, t_new, d),
            jnp.stack(kp_l, axis=1), jnp.stack(vp_l, axis=1), jnp.stack(ikp_l, axis=1),
            jnp.stack(ks_l, axis=1), jnp.stack(vs_l, axis=1), jnp.stack(iks_l, axis=1),
            jnp.stack(cv_l, axis=1))
```

```python
import functools

import jax
import jax.numpy as jnp
from jax import lax
from jax.experimental import pallas as pl
from jax.experimental.pallas import tpu as pltpu

EPS = 1e-6
ROPE_THETA = 10000.0
CHUNK = 128
A_GROUPS = 8
N_HEADS = 8
N_KV_HEADS = 2
IDX_HEADS = 8
IDX_DIM = 64
TOPK_MAX = 256
PAGE_SIZE = 128

LANES = 128
SUBLANES = 8
INT_MIN = -(2 ** 31)
F32_MAX_BITS = 0x7F7FFFFF
Q_SCALE = LANES ** -0.5 * 1.4426950408889634
NEG = -1e30
VMEM_LIMIT = 56 * 1024 * 1024

F32 = jnp.float32
BF16 = jnp.bfloat16
I32 = jnp.int32


def _bf(x):
    return x.astype(BF16)


def _dot(a, b):
    return jnp.dot(a, b, preferred_element_type=F32)


def _dot_nt(a, b):
    return lax.dot_general(a, b, (((1,), (1,)), ((), ())), preferred_element_type=F32)


def _rms(x, g):
    r = lax.rsqrt(jnp.mean(x * x, axis=-1, keepdims=True) + EPS)
    return x * r * g


def _silu(z):
    return z * jax.nn.sigmoid(z)


def _per_layer_embed(x1, p, gpe, wg, wpe):
    gate = jax.nn.sigmoid(_dot(_bf(_rms(x1, gpe)), wg))
    return x1 + gate * _dot(_bf(p), wpe)


def _key_to_f32(key):
    kc = jnp.clip(key, -F32_MAX_BITS, F32_MAX_BITS)
    return pltpu.bitcast(jnp.where(kc < 0, (-kc) | INT_MIN, kc), F32)


def _topk_threshold(count_ge, shape, kk):
    def one_pass(j, carry):
        key, cnt_key = carry
        cand = key + (jnp.int32(1) << (31 - j))
        cnt = count_ge(_key_to_f32(cand))
        ok = cnt >= kk
        return jnp.where(ok, cand, key), jnp.where(ok, cnt, cnt_key)

    key, cnt = lax.fori_loop(0, 32, one_pass, (jnp.full(shape, INT_MIN, I32), jnp.zeros(shape, F32)))
    return _key_to_f32(key), cnt


def _tie_cutoff(count_tie_before, shape, quota, nbits):
    def idx_pass(j, lim):
        cand = lim + (jnp.int32(1) << (nbits - 1 - j))
        return jnp.where(count_tie_before(cand) < quota, cand, lim)

    return lax.fori_loop(0, nbits, idx_pass, jnp.zeros(shape, I32))


def _const_spec(shape):
    nd = len(shape)
    return pl.BlockSpec(shape, lambda *_: (0,) * nd, pipeline_mode=pl.Buffered(1))


def _stacked_spec(w, index):
    return pl.BlockSpec((None,) + w.shape[1:], lambda *_: (index, 0, 0), pipeline_mode=pl.Buffered(1))


def _params(sem):
    return pltpu.CompilerParams(dimension_semantics=sem, vmem_limit_bytes=VMEM_LIMIT)


def _layer_a_kernel(x_ref, p_ref, gn_ref, win_ref, ws_ref, bs_ref, gv_ref, wout_ref,
                    gpe_ref, wg_ref, wpe_ref, xo_ref, *v_out, chunk):
    aw = gv_ref.shape[1]
    gd = aw // A_GROUPS
    x = x_ref[...]
    tm = x.shape[0]
    h = _bf(_rms(x, gn_ref[...]))
    u = _dot(h, win_ref[:, 0:aw])
    v = _rms(_dot(h, win_ref[:, aw:2 * aw]), gv_ref[...])
    z = _dot(h, win_ref[:, 2 * aw:3 * aw])
    if v_out:
        v_out[0][...] = v
    vb = _bf(v)
    row = lax.broadcasted_iota(I32, (CHUNK, CHUNK), 0)
    col = lax.broadcasted_iota(I32, (CHUNK, CHUNK), 1)
    keep = col <= row
    if chunk != CHUNK:
        keep = jnp.logical_and(keep, (row // chunk) == (col // chunk))
    cols = []
    for g in range(A_GROUPS):
        wsg = _bf(jnp.where(keep, ws_ref[g], 0.0))
        rows = [_dot(wsg, vb[j * CHUNK:(j + 1) * CHUNK, g * gd:(g + 1) * gd]) for j in range(tm // CHUNK)]
        cols.append(rows[0] if len(rows) == 1 else jnp.concatenate(rows, axis=0))
    mixed = jnp.concatenate(cols, axis=1)
    bias = bs_ref[...]
    if tm != CHUNK:
        bias = jnp.concatenate([bias] * (tm // CHUNK), axis=0)
    y = u * (mixed + bias) * _silu(z)
    x1 = x + _dot(_bf(y), wout_ref[...])
    xo_ref[...] = _per_layer_embed(x1, p_ref[...], gpe_ref[...], wg_ref[...], wpe_ref[...])


def _layer_p_spec(p, layer, rows, row_index):
    return pl.BlockSpec((None, rows, p.shape[2]), lambda *g: (layer, row_index(*g), 0))


def _layer_a(x, p, gn, win, ws_t, bs_t, gv, wout, gpe, wg, wpe, *, layer, mixer, chunk, tm, want_v):
    n, d = x.shape
    aw = gv.shape[1]
    pd = p.shape[2]
    row_spec = lambda w: pl.BlockSpec((tm, w), lambda i: (i, 0))
    out_shape = [jax.ShapeDtypeStruct((n, d), F32)]
    out_specs = [row_spec(d)]
    if want_v:
        out_shape.append(jax.ShapeDtypeStruct((n, aw), F32))
        out_specs.append(row_spec(aw))
    res = pl.pallas_call(
        functools.partial(_layer_a_kernel, chunk=chunk),
        grid=(n // tm,),
        in_specs=[row_spec(d), _layer_p_spec(p, layer, tm, lambda i: i), _const_spec((1, d)), _stacked_spec(win, mixer),
                  _const_spec((A_GROUPS, CHUNK, CHUNK)), _const_spec((CHUNK, aw)), _const_spec((1, aw)),
                  _stacked_spec(wout, mixer), _const_spec((1, d)), _stacked_spec(wg, layer), _stacked_spec(wpe, layer)],
        out_specs=out_specs,
        out_shape=out_shape,
        compiler_params=_params(("arbitrary",)),
        name="layer_a",
    )(x, p, gn, win, ws_t, bs_t, gv, wout, gpe, wg, wpe)
    return res if want_v else (res[0], None)


def _proj_b_kernel(x_ref, gn_ref, wqkv_ref, wiq_ref, wz_ref, wsmall_ref, wiwt_ref, gq_ref, gk_ref, gik_ref,
                   c128_ref, s128_ref, c64_ref, s64a_ref, s64b_ref,
                   q_ref, k_ref, v_ref, ik_ref, kbf_ref, vt_ref, ikbf_ref, iq_ref, iw_ref, iwt_ref, sz_ref, *, kc):
    dh = LANES
    bw = N_HEADS * dh
    kvw = N_KV_HEADS * dh
    iqw = IDX_HEADS * IDX_DIM
    x = x_ref[...]
    tm = x.shape[0]
    h = _bf(_rms(x, gn_ref[...]))
    small = _dot(h, wsmall_ref[...])
    iwt_ref[...] = _dot_nt(wiwt_ref[...], h)
    m_iq = _dot(h, wiq_ref[...])
    m_q = _dot(h, wqkv_ref[:, 0:bw])
    m_k = _dot(h, wqkv_ref[:, bw:bw + kvw])
    v = _dot(h, wqkv_ref[:, bw + kvw:bw + 2 * kvw])
    m_z = _dot(h, wz_ref[...])
    c128, s128 = c128_ref[...], s128_ref[...]
    c64, s64a, s64b = c64_ref[...], s64a_ref[...], s64b_ref[...]

    def rope128(t):
        return t * c128 + pltpu.roll(t, dh // 2, 1) * s128

    def rope64(t):
        return t * c64 + pltpu.roll(t, LANES - IDX_DIM // 2, 1) * s64a + pltpu.roll(t, IDX_DIM // 2, 1) * s64b

    lane = lax.broadcasted_iota(I32, (tm, LANES), 1)
    ms = jnp.sum(jnp.where(lane < IDX_DIM, small * small, 0.0), axis=-1, keepdims=True) * (1.0 / IDX_DIM)
    ikr = rope64(small * lax.rsqrt(ms + EPS) * gik_ref[...])[:, :IDX_DIM]
    ik_ref[...] = ikr
    ikbf_ref[...] = _bf(ikr)
    iw_ref[...] = small[:, IDX_DIM:IDX_DIM + IDX_HEADS]
    for j in range(iqw // LANES):
        iq_ref[:, j * LANES:(j + 1) * LANES] = _bf(rope64(m_iq[:, j * LANES:(j + 1) * LANES]))
    for hh in range(N_HEADS):
        qh = rope128(_rms(m_q[:, hh * dh:(hh + 1) * dh], gq_ref[...]))
        q_ref[:, hh * dh:(hh + 1) * dh] = _bf(qh * Q_SCALE)
    for hh in range(N_KV_HEADS):
        kh = rope128(_rms(m_k[:, hh * dh:(hh + 1) * dh], gk_ref[...]))
        k_ref[pl.ds(hh, tm, stride=N_KV_HEADS), :] = kh
        v_ref[pl.ds(hh, tm, stride=N_KV_HEADS), :] = v[:, hh * dh:(hh + 1) * dh]
        kbf_ref[:, hh * dh:(hh + 1) * dh] = _bf(kh)
    for j in range(tm // kc):
        vt_ref[j] = _bf(v[j * kc:(j + 1) * kc, :].T)
    sz_ref[...] = _silu(m_z)


def _proj_b(x, gn, win_b, wz, wsmall, wiwt, gq, gk, gik, tabs, *, mixer, tm, kc):
    n, d = x.shape
    bw = N_HEADS * LANES
    kvw = N_KV_HEADS * LANES
    iqw = IDX_HEADS * IDX_DIM
    qkvw = bw + 2 * kvw
    assert qkvw % iqw == 0
    period = tabs[0].shape[0] // tm
    row_spec = lambda w: pl.BlockSpec((tm, w), lambda i: (i, 0))
    tab_spec = pl.BlockSpec((tm, LANES), lambda i: (i % period, 0))
    sds = jax.ShapeDtypeStruct
    outs = [
        (sds((n, bw), BF16), row_spec(bw)),
        (sds((n * N_KV_HEADS, LANES), F32), pl.BlockSpec((tm * N_KV_HEADS, LANES), lambda i: (i, 0))),
        (sds((n * N_KV_HEADS, LANES), F32), pl.BlockSpec((tm * N_KV_HEADS, LANES), lambda i: (i, 0))),
        (sds((n, IDX_DIM), F32), row_spec(IDX_DIM)),
        (sds((n, kvw), BF16), row_spec(kvw)),
        (sds((n // kc, kvw, kc), BF16), pl.BlockSpec((tm // kc, kvw, kc), lambda i: (i, 0, 0))),
        (sds((n, IDX_DIM), BF16), row_spec(IDX_DIM)),
        (sds((n, iqw), BF16), row_spec(iqw)),
        (sds((n, IDX_HEADS), F32), row_spec(IDX_HEADS)),
        (sds((IDX_HEADS, n), F32), pl.BlockSpec((IDX_HEADS, tm), lambda i: (0, i))),
        (sds((n, bw), F32), row_spec(bw)),
    ]
    return pl.pallas_call(
        functools.partial(_proj_b_kernel, kc=kc),
        grid=(n // tm,),
        in_specs=[row_spec(d), _const_spec((1, d)),
                  pl.BlockSpec((None, d, qkvw), lambda i: (mixer, 0, 0), pipeline_mode=pl.Buffered(1)),
                  pl.BlockSpec((None, d, iqw), lambda i: (mixer, 0, qkvw // iqw), pipeline_mode=pl.Buffered(1)),
                  _const_spec(wz.shape), _const_spec(wsmall.shape),
                  _const_spec(wiwt.shape), _const_spec((1, LANES)), _const_spec((1, LANES)), _const_spec((1, LANES)),
                  tab_spec, tab_spec, tab_spec, tab_spec, tab_spec],
        out_specs=[o[1] for o in outs],
        out_shape=[o[0] for o in outs],
        compiler_params=_params(("arbitrary",)),
        name="proj_b",
    )(x, gn, win_b, win_b, wz, wsmall, wiwt, gq, gk, gik, *tabs)


def _dsa_prompt_sample_kernel(pt_ref, *refs, topk, sample):
    prompt_in, sample_in = refs[:13], refs[13:20]
    xo_ref, y_ref = refs[20:22]
    prompt_scratch, sample_scratch = refs[22:26], refs[26:30]
    step = pl.program_id(0) * pl.num_programs(1) + pl.program_id(1)
    nsteps = pl.num_programs(0) * pl.num_programs(1)
    _sample_attention_step(pt_ref, *sample_in, y_ref, *sample_scratch, step, nsteps, **sample)
    _dsa_prompt_kernel(*prompt_in, xo_ref, *prompt_scratch, topk=topk)


def _dsa_prompt_kernel(iq_ref, iwt_ref, ikbf_ref, kbf_ref, vt_ref, q_ref, sz_ref, x_ref, p_ref,
                       wout_ref, gpe_ref, wg_ref, wpe_ref, xo_ref, sc_ref, m_sc, l_sc, acc_sc, *, topk):
    tq = q_ref.shape[0]
    s_len = sc_ref.shape[0]
    dh = LANES
    i = pl.program_id(1)
    nch = i + 1
    qpos = i * tq + lax.broadcasted_iota(I32, (1, tq), 1)
    krow = lax.broadcasted_iota(I32, (tq, tq), 0)
    idx_scale = IDX_DIM ** -0.5 * IDX_HEADS ** -0.5

    def chunk_start(c):
        return pl.multiple_of(c * tq, tq)

    def idx_dots(c):
        ikc = ikbf_ref[pl.ds(chunk_start(c), tq), :]
        return [_dot_nt(ikc, iq_ref[:, h * IDX_DIM:(h + 1) * IDX_DIM]) for h in range(IDX_HEADS)]

    def idx_combine(c, xs):
        off = chunk_start(c)
        s = jnp.maximum(xs[0], 0.0) * iwt_ref[0:1, :]
        for h in range(1, IDX_HEADS):
            s = s + jnp.maximum(xs[h], 0.0) * iwt_ref[h:h + 1, :]
        sc_ref[pl.ds(off, tq), :] = jnp.where(off + krow <= qpos, s * idx_scale, -jnp.inf)

    def build_pair(j, carry):
        xs0 = idx_dots(2 * j)
        xs1 = idx_dots(2 * j + 1)
        idx_combine(2 * j, xs0)
        idx_combine(2 * j + 1, xs1)
        return carry

    lax.fori_loop(0, nch // 2, build_pair, 0)

    @pl.when(nch % 2 == 1)
    def _():
        idx_combine(nch - 1, idx_dots(nch - 1))

    def count(pred):
        def body(c, acc):
            off = chunk_start(c)
            m = jnp.where(pred(sc_ref[pl.ds(off, tq), :], off + krow), 1.0, 0.0)
            part = m[0:32]
            for r in range(1, tq // 32):
                part = part + m[r * 32:(r + 1) * 32]
            return acc + part
        acc = lax.fori_loop(0, nch, body, jnp.zeros((32, tq), F32))
        return jnp.sum(acc, axis=0, keepdims=True)

    kk = float(topk)
    t_thr, cnt_thr = _topk_threshold(lambda t: count(lambda blk, pos: blk >= t), (1, tq), kk)

    need = cnt_thr > kk

    @pl.when(jnp.max(jnp.where(need, 1.0, 0.0)) > 0.0)
    def _():
        quota = kk - count(lambda blk, pos: blk > t_thr)
        last = _tie_cutoff(lambda cand: count(lambda blk, pos: jnp.logical_and(blk == t_thr, pos < cand)),
                           (1, tq), quota, max(1, (s_len - 1).bit_length()))

        def demote(c, carry):
            off = chunk_start(c)
            blk = sc_ref[pl.ds(off, tq), :]
            drop = jnp.logical_and(jnp.logical_and(blk == t_thr, off + krow > last), need)
            sc_ref[pl.ds(off, tq), :] = jnp.where(drop, -jnp.inf, blk)
            return carry

        lax.fori_loop(0, nch, demote, 0)

    m_sc[...] = jnp.full(m_sc.shape, NEG, F32)
    l_sc[...] = jnp.zeros(l_sc.shape, F32)
    acc_sc[...] = jnp.zeros(acc_sc.shape, F32)
    rep = N_HEADS // N_KV_HEADS
    sub = SUBLANES

    def qk_dots(c):
        kc_all = kbf_ref[pl.ds(chunk_start(c), tq), :]
        return [_dot_nt(kc_all[:, (h // rep) * dh:(h // rep + 1) * dh], q_ref[:, h * dh:(h + 1) * dh])
                for h in range(N_HEADS)]

    def softmax_pv(c, qk):
        bias = jnp.where(sc_ref[pl.ds(chunk_start(c), tq), :] >= t_thr, 0.0, NEG)
        for g in range(N_KV_HEADS):
            vg = vt_ref[c, g * dh:(g + 1) * dh, :]
            for r in range(rep):
                h = g * rep + r
                st = slice(h * sub, (h + 1) * sub)
                s = (bias + qk[h]).reshape(tq // sub, sub, tq)
                m_old = m_sc[st, :]
                m_new = jnp.maximum(m_old, jnp.max(jnp.max(s, axis=0), axis=0, keepdims=True))
                p = jnp.exp2(s - m_new[None])
                alpha = jnp.exp2(m_old - m_new)
                l_sc[st, :] = alpha * l_sc[st, :] + jnp.sum(p, axis=0)
                acc_sc[h * dh:(h + 1) * dh, :] = (alpha[0:1] * acc_sc[h * dh:(h + 1) * dh, :]
                                                  + _dot(vg, _bf(p.reshape(tq, tq))))
                m_sc[st, :] = m_new

    def attend_pair(j, carry):
        qk0 = qk_dots(2 * j)
        qk1 = qk_dots(2 * j + 1)
        softmax_pv(2 * j, qk0)
        softmax_pv(2 * j + 1, qk1)
        return carry

    lax.fori_loop(0, nch // 2, attend_pair, 0)

    @pl.when(nch % 2 == 1)
    def _():
        softmax_pv(nch - 1, qk_dots(nch - 1))

    inv = [1.0 / jnp.sum(l_sc[h * sub:(h + 1) * sub, :], axis=0, keepdims=True) for h in range(N_HEADS)]
    att_t = jnp.concatenate([acc_sc[h * dh:(h + 1) * dh, :] * inv[h] for h in range(N_HEADS)], axis=0)
    y = _bf(att_t.T * sz_ref[...])
    x1 = x_ref[...] + _dot(y, wout_ref[...])
    xo_ref[...] = _per_layer_embed(x1, p_ref[...], gpe_ref[...], wg_ref[...], wpe_ref[...])


def _dsa_prompt(pr, x, p, wout, gpe, wg, wpe, *, layer, mixer, bsz, s_len, tq, topk, sample=None):
    q, _, _, _, kbf, vt, ikbf, iq, _, iwt, sz = pr
    n, d = x.shape
    bw = q.shape[1]
    kvw = kbf.shape[1]
    nq = s_len // tq
    blk = lambda w: pl.BlockSpec((tq, w), lambda b, i, *_: (b * nq + i, 0))
    per_b = lambda w: pl.BlockSpec((s_len, w), lambda b, i, *_: (b, 0))
    in_specs = [blk(iq.shape[1]),
                pl.BlockSpec((IDX_HEADS, tq), lambda b, i, *_: (0, b * nq + i)),
                per_b(IDX_DIM), per_b(kvw),
                pl.BlockSpec((nq, kvw, tq), lambda b, i, *_: (b, 0, 0)),
                blk(bw), blk(bw), blk(d), _layer_p_spec(p, layer, tq, lambda b, i, *_: b * nq + i),
                _stacked_spec(wout, mixer), _const_spec((1, d)), _stacked_spec(wg, layer), _stacked_spec(wpe, layer)]
    scratch = [pltpu.VMEM((s_len, tq), F32), pltpu.VMEM((N_HEADS * SUBLANES, tq), F32),
               pltpu.VMEM((N_HEADS * SUBLANES, tq), F32), pltpu.VMEM((bw, tq), F32)]
    args = (iq, iwt, ikbf, kbf, vt, q, sz, x, p, wout, gpe, wg, wpe)
    if sample is None:
        return pl.pallas_call(
            functools.partial(_dsa_prompt_kernel, topk=topk),
            grid=(bsz, nq), in_specs=in_specs, out_specs=blk(d),
            out_shape=jax.ShapeDtypeStruct((n, d), F32), scratch_shapes=scratch,
            compiler_params=_params(("arbitrary", "arbitrary")), name="dsa_prompt",
        )(*args)

    page_table, q_s, bias, k_new, v_new, sz_s, cache_k, cache_v, cache_layer, group, t_new = sample
    n_pages = page_table.shape[1]
    lp = n_pages * PAGE_SIZE + LANES
    rows = group * t_new
    sblk = lambda r, w: pl.BlockSpec((r, w), lambda b, i, *_: (b * nq + i, 0))
    any_spec = pl.BlockSpec(memory_space=pl.ANY)
    return pl.pallas_call(
        functools.partial(_dsa_prompt_sample_kernel, topk=topk,
                          sample=dict(group=group, n_pages=n_pages, layer=cache_layer, t_new=t_new)),
        grid_spec=pltpu.PrefetchScalarGridSpec(
            num_scalar_prefetch=1,
            grid=(bsz, nq),
            in_specs=in_specs + [sblk(rows, bw), sblk(rows, lp), sblk(rows * N_KV_HEADS, LANES),
                                 sblk(rows * N_KV_HEADS, LANES), sblk(rows, bw), any_spec, any_spec],
            out_specs=[blk(d), sblk(rows, bw)],
            scratch_shapes=scratch + [pltpu.VMEM((2, group, lp * N_KV_HEADS, LANES), F32),
                                      pltpu.VMEM((2, group, lp * N_KV_HEADS, LANES), F32),
                                      pltpu.SemaphoreType.DMA((2,)), pltpu.SemaphoreType.DMA((2,))]),
        out_shape=[jax.ShapeDtypeStruct((n, d), F32), jax.ShapeDtypeStruct(q_s.shape, F32)],
        compiler_params=_params(("arbitrary", "arbitrary")),
        name="dsa_prompt_sample",
    )(page_table, *args, q_s, bias, k_new, v_new, sz_s, cache_k, cache_v)


def _page_copies(pt_ref, cache_ref, buf, sem, step, slot, *, group, n_pages, layer, rows_per_page, axis):
    copies = []
    for b in range(group):
        for n in range(n_pages):
            page = pt_ref[step * group + b, n]
            span = pl.ds(n * rows_per_page, rows_per_page)
            dst = buf.at[slot, b, span, :] if axis == 0 else buf.at[slot, b, :, span]
            copies.append(pltpu.make_async_copy(cache_ref.at[page, layer], dst, sem.at[slot]))
    return copies


def _paged_pipeline(make_copies_list, step, nsteps):
    slot = step % 2

    @pl.when(step == 0)
    def _():
        for mk in make_copies_list:
            for cp in mk(step, slot):
                cp.start()

    @pl.when(step + 1 < nsteps)
    def _():
        for mk in make_copies_list:
            for cp in mk(step + 1, 1 - slot):
                cp.start()

    for mk in make_copies_list:
        for cp in mk(step, slot):
            cp.wait()
    return slot


def _idx_sample_kernel(pt_ref, iq_ref, iw_ref, iknew_ref, cik_ref, bias_ref, ikbuf, sem, sc_ref,
                       *, group, n_pages, layer, t_new, topk):
    past = n_pages * PAGE_SIZE
    lp = past + LANES
    slot = _paged_pipeline([functools.partial(_page_copies, pt_ref, cik_ref, ikbuf, sem, group=group,
                                              n_pages=n_pages, layer=layer, rows_per_page=PAGE_SIZE, axis=1)],
                           pl.program_id(0), pl.num_programs(0))
    idx_scale = IDX_DIM ** -0.5 * IDX_HEADS ** -0.5
    iq_all = iq_ref[...].astype(F32)
    iknew_all = iknew_ref[...].astype(F32)
    iw_all = iw_ref[...]
    kpos = lax.broadcasted_iota(I32, (t_new, lp), 1)
    qpos = past + lax.broadcasted_iota(I32, (t_new, lp), 0)
    dots = []
    for b in range(group):
        rows = slice(b * t_new, (b + 1) * t_new)
        iqb = iq_all[rows]
        iqs = _bf(jnp.concatenate([iqb[:, h * IDX_DIM:(h + 1) * IDX_DIM] for h in range(IDX_HEADS)], axis=0))
        ik_new = _bf(jnp.concatenate([iknew_all[rows], jnp.zeros((LANES - t_new, IDX_DIM), F32)], axis=0))
        dots.append((_dot(iqs, _bf(ikbuf[slot, b])), _dot_nt(iqs, ik_new)))
    for b in range(group):
        rows = slice(b * t_new, (b + 1) * t_new)
        xp, xn = dots[b]
        iwb = iw_all[rows]
        sp = jnp.zeros((t_new, past), F32)
        sn = jnp.zeros((t_new, LANES), F32)
        for h in range(IDX_HEADS):
            w = iwb[:, h:h + 1]
            sp = sp + jnp.maximum(xp[h * t_new:(h + 1) * t_new], 0.0) * w
            sn = sn + jnp.maximum(xn[h * t_new:(h + 1) * t_new], 0.0) * w
        sc = jnp.concatenate([sp, sn], axis=1) * idx_scale
        sc_ref[rows, :] = jnp.where(kpos <= qpos, sc, -jnp.inf)

    kk = float(topk)
    rows_all = group * t_new
    kpos_all = lax.broadcasted_iota(I32, (rows_all, lp), 1)

    def count(pred):
        return jnp.sum(jnp.where(pred(sc_ref[...]), 1.0, 0.0), axis=1, keepdims=True)

    t_thr, cnt_thr = _topk_threshold(lambda t: count(lambda blk: blk >= t), (rows_all, 1), kk)
    need = cnt_thr > kk

    @pl.when(jnp.max(jnp.where(need, 1.0, 0.0)) > 0.0)
    def _():
        quota = kk - count(lambda blk: blk > t_thr)
        last = _tie_cutoff(lambda cand: count(lambda blk: jnp.logical_and(blk == t_thr, kpos_all < cand)),
                           (rows_all, 1), quota, max(1, (lp - 1).bit_length()))
        blk = sc_ref[...]
        drop = jnp.logical_and(jnp.logical_and(blk == t_thr, kpos_all > last), need)
        sc_ref[...] = jnp.where(drop, -jnp.inf, blk)

    bias_ref[...] = jnp.where(sc_ref[...] >= t_thr, 0.0, NEG)


def _idx_sample(page_table, iq, iw, ikbf, cache_ik, *, layer, group, t_new, topk):
    n = iq.shape[0]
    db, n_pages = page_table.shape
    past = n_pages * PAGE_SIZE
    lp = past + LANES
    rows = group * t_new
    blk = lambda w: pl.BlockSpec((rows, w), lambda i, pt: (i, 0))
    return pl.pallas_call(
        functools.partial(_idx_sample_kernel, group=group, n_pages=n_pages, layer=layer, t_new=t_new, topk=topk),
        grid_spec=pltpu.PrefetchScalarGridSpec(
            num_scalar_prefetch=1,
            grid=(db // group,),
            in_specs=[blk(iq.shape[1]), blk(IDX_HEADS), blk(IDX_DIM), pl.BlockSpec(memory_space=pl.ANY)],
            out_specs=blk(lp),
            scratch_shapes=[pltpu.VMEM((2, group, IDX_DIM, past), F32), pltpu.SemaphoreType.DMA((2,)),
                            pltpu.VMEM((rows, lp), F32)]),
        out_shape=jax.ShapeDtypeStruct((n, lp), F32),
        compiler_params=_params(("arbitrary",)),
        name="idx_sample",
    )(page_table, iq, iw, ikbf, cache_ik)


def _attn_sample_kernel(pt_ref, q_ref, bias_ref, knew_ref, vnew_ref, sz_ref, ck_ref, cv_ref, y_ref,
                        kbuf, vbuf, ksem, vsem, **static):
    _sample_attention_step(pt_ref, q_ref, bias_ref, knew_ref, vnew_ref, sz_ref, ck_ref, cv_ref, y_ref,
                           kbuf, vbuf, ksem, vsem, pl.program_id(0), pl.num_programs(0), **static)


def _sample_attention_step(pt_ref, q_ref, bias_ref, knew_ref, vnew_ref, sz_ref, ck_ref, cv_ref, y_ref,
                           kbuf, vbuf, ksem, vsem, step, nsteps, *, group, n_pages, layer, t_new):
    past = n_pages * PAGE_SIZE
    lp = past + LANES
    dh = LANES
    nkv = N_KV_HEADS
    rep = N_HEADS // nkv
    copy_args = dict(group=group, n_pages=n_pages, layer=layer, rows_per_page=PAGE_SIZE * nkv, axis=0)
    slot = _paged_pipeline([functools.partial(_page_copies, pt_ref, ck_ref, kbuf, ksem, **copy_args),
                            functools.partial(_page_copies, pt_ref, cv_ref, vbuf, vsem, **copy_args)],
                           step, nsteps)
    q_all = q_ref[...].astype(F32)
    pad = jnp.zeros(((LANES - t_new) * nkv, dh), F32)
    qk = {}
    for b in range(group):
        rows = slice(b * t_new, (b + 1) * t_new)
        new_rows = slice(b * t_new * nkv, (b + 1) * t_new * nkv)
        tail = pl.ds(past * nkv, LANES * nkv)
        kbuf[slot, b, tail, :] = jnp.concatenate([knew_ref[new_rows, :], pad], axis=0)
        vbuf[slot, b, tail, :] = jnp.concatenate([vnew_ref[new_rows, :], pad], axis=0)
        qb = q_all[rows]
        for g in range(nkv):
            kg = _bf(kbuf[slot, b, pl.ds(g, lp, stride=nkv), :])
            qg = _bf(jnp.concatenate([qb[:, (g * rep + r) * dh:(g * rep + r + 1) * dh] for r in range(rep)], axis=0))
            qk[b, g] = _dot_nt(qg, kg)
    for b in range(group):
        rows = slice(b * t_new, (b + 1) * t_new)
        bias = jnp.concatenate([bias_ref[rows, :]] * rep, axis=0)
        outs = []
        for g in range(nkv):
            vg = _bf(vbuf[slot, b, pl.ds(g, lp, stride=nkv), :])
            s = bias + qk[b, g]
            m = jnp.max(s, axis=1, keepdims=True)
            p = jnp.exp2(s - m)
            l = jnp.sum(p, axis=1, keepdims=True)
            o = _dot(_bf(p), vg) * (1.0 / l)
            outs.extend(o[r * t_new:(r + 1) * t_new] for r in range(rep))
        y_ref[rows, :] = jnp.concatenate(outs, axis=1) * sz_ref[rows, :]


def _attn_sample(page_table, q, bias, knew, vnew, sz, cache_k, cache_v, *, layer, group, t_new):
    n, bw = q.shape
    db, n_pages = page_table.shape
    past = n_pages * PAGE_SIZE
    lp = past + LANES
    rows = group * t_new
    blk = lambda w: pl.BlockSpec((rows, w), lambda i, pt: (i, 0))
    new_blk = pl.BlockSpec((rows * N_KV_HEADS, LANES), lambda i, pt: (i, 0))
    return pl.pallas_call(
        functools.partial(_attn_sample_kernel, group=group, n_pages=n_pages, layer=layer, t_new=t_new),
        grid_spec=pltpu.PrefetchScalarGridSpec(
            num_scalar_prefetch=1,
            grid=(db // group,),
            in_specs=[blk(bw), blk(lp), new_blk, new_blk, blk(bw),
                      pl.BlockSpec(memory_space=pl.ANY), pl.BlockSpec(memory_space=pl.ANY)],
            out_specs=blk(bw),
            scratch_shapes=[pltpu.VMEM((2, group, lp * N_KV_HEADS, LANES), F32),
                            pltpu.VMEM((2, group, lp * N_KV_HEADS, LANES), F32),
                            pltpu.SemaphoreType.DMA((2,)), pltpu.SemaphoreType.DMA((2,))]),
        out_shape=jax.ShapeDtypeStruct((n, bw), F32),
        compiler_params=_params(("arbitrary",)),
        name="attn_sample",
    )(page_table, q, bias, knew, vnew, sz, cache_k, cache_v)


def _out_ple_kernel(y_ref, x_ref, p_ref, wout_ref, gpe_ref, wg_ref, wpe_ref, xo_ref):
    x1 = x_ref[...] + _dot(_bf(y_ref[...]), wout_ref[...])
    xo_ref[...] = _per_layer_embed(x1, p_ref[...], gpe_ref[...], wg_ref[...], wpe_ref[...])


def _out_ple(y, x, p, wout, gpe, wg, wpe, *, layer, mixer, tm):
    n, d = x.shape
    row_spec = lambda w: pl.BlockSpec((tm, w), lambda i: (i, 0))
    return pl.pallas_call(
        _out_ple_kernel,
        grid=(n // tm,),
        in_specs=[row_spec(y.shape[1]), row_spec(d), _layer_p_spec(p, layer, tm, lambda i: i),
                  _stacked_spec(wout, mixer), _const_spec((1, d)), _stacked_spec(wg, layer), _stacked_spec(wpe, layer)],
        out_specs=row_spec(d),
        out_shape=jax.ShapeDtypeStruct((n, d), F32),
        compiler_params=_params(("arbitrary",)),
        name="out_ple",
    )(y, x, p, wout, gpe, wg, wpe)


def _rope_tables(pos):
    def angles(half):
        inv = ROPE_THETA ** (-jnp.arange(half, dtype=F32) / half)
        return pos.astype(F32)[:, None] * inv[None, :]
    a = angles(LANES // 2)
    c128 = jnp.concatenate([jnp.cos(a), jnp.cos(a)], axis=1)
    s128 = jnp.concatenate([-jnp.sin(a), jnp.sin(a)], axis=1)
    a = angles(IDX_DIM // 2)
    zero = jnp.zeros_like(a)
    c64 = jnp.tile(jnp.cos(a), (1, 4))
    s64a = jnp.tile(jnp.concatenate([-jnp.sin(a), zero], axis=1), (1, 2))
    s64b = jnp.tile(jnp.concatenate([zero, jnp.sin(a)], axis=1), (1, 2))
    return c128, s128, c64, s64a, s64b


def _tile_rows(n, pref):
    t = min(n, pref)
    assert n % t == 0, (n, t)
    return t


def kernel(x_prompt, x_sample, cache_k, cache_v, cache_ik, page_table, p_prompt, p_sample, g_norm, w_in_a, w_s_a, b_s_a, g_v_a, w_out_a, w_in_b, g_q_b, g_k_b, g_ik_b, w_out_b, g_pe, w_pe_gate, w_pe_proj):
    bsz, s_len, d = x_prompt.shape
    db, t_new, _ = x_sample.shape
    depth = g_norm.shape[0]
    n_pool, nb_layers = cache_k.shape[0], cache_k.shape[1]
    n_pages = page_table.shape[1]
    past = n_pages * PAGE_SIZE
    aw = g_v_a.shape[1]
    bw = N_HEADS * LANES
    kvw = N_KV_HEADS * LANES
    iqw = IDX_HEADS * IDX_DIM
    assert d == bw and s_len % CHUNK == 0 and CHUNK % t_new == 0 and t_new % 8 == 0
    assert w_in_b.shape[2] == 2 * bw + 2 * kvw + iqw + IDX_DIM + IDX_HEADS

    n_p, n_s = bsz * s_len, db * t_new
    yp = x_prompt.reshape(n_p, d)
    ys = x_sample.reshape(n_s, d)
    pp = p_prompt.reshape(depth, n_p, -1)
    ps = p_sample.reshape(depth, n_s, -1)
    ck = cache_k.reshape(n_pool, nb_layers, PAGE_SIZE * N_KV_HEADS, LANES)
    cv = cache_v.reshape(n_pool, nb_layers, PAGE_SIZE * N_KV_HEADS, LANES)
    cik = jnp.swapaxes(cache_ik, 2, 3)

    tm_p = _tile_rows(n_p, 256)
    tm_s = _tile_rows(n_s, 256)
    tq = _tile_rows(s_len, 256)
    topk_p = min(TOPK_MAX, s_len // 4)
    topk_s = min(TOPK_MAX, (past + t_new) // 4)
    idx_group = _tile_rows(db, 16)
    att_group = _tile_rows(db, 2)

    tabs_p = _rope_tables(jnp.arange(s_len))
    tabs_s = _rope_tables(past + (jnp.arange(tm_s) % t_new))

    gn = g_norm.reshape(depth, 1, d)
    gpe = g_pe.reshape(depth, 1, d)
    wg = _bf(w_pe_gate)
    wpe = _bf(w_pe_proj)
    win_a = _bf(w_in_a)
    wout_a = _bf(w_out_a)
    wout_b = _bf(w_out_b)
    win_b = _bf(w_in_b)

    kp_l, vp_l, ikp_l, ks_l, vs_l, iks_l, cv_l = [], [], [], [], [], [], []
    for i in range(depth):
        j = i // 2
        if i % 2 == 0:
            gv = g_v_a[j].reshape(1, aw)
            reps = CHUNK // t_new
            ws_p = w_s_a[j]
            bs_p = jnp.repeat(b_s_a[j].T, aw // A_GROUPS, axis=1)
            ws_s = jnp.tile(w_s_a[j][:, :t_new, :t_new], (1, reps, reps))
            bs_s = jnp.tile(jnp.repeat(b_s_a[j][:, :t_new].T, aw // A_GROUPS, axis=1), (reps, 1))
            yp, _ = _layer_a(yp, pp, gn[i], win_a, ws_p, bs_p, gv, wout_a, gpe[i], wg, wpe,
                             layer=i, mixer=j, chunk=CHUNK, tm=tm_p, want_v=False)
            ys, v_rows = _layer_a(ys, ps, gn[i], win_a, ws_s, bs_s, gv, wout_a, gpe[i], wg, wpe,
                                  layer=i, mixer=j, chunk=t_new, tm=tm_s, want_v=True)
            cv_l.append(v_rows.reshape(db, t_new, aw))
        else:
            w = w_in_b[j]
            o_iq = bw + 2 * kvw
            o_ik = o_iq + iqw
            o_iw = o_ik + IDX_DIM
            o_z = o_iw + IDX_HEADS
            wz = _bf(w[:, o_z:])
            wsmall = _bf(jnp.pad(w[:, o_ik:o_z], ((0, 0), (0, LANES - IDX_DIM - IDX_HEADS))))
            wiwt = _bf(w[:, o_iw:o_z].T)
            gq = g_q_b[j].reshape(1, LANES)
            gk = g_k_b[j].reshape(1, LANES)
            gik = jnp.pad(g_ik_b[j], (0, LANES - IDX_DIM)).reshape(1, LANES)

            pr = _proj_b(yp, gn[i], win_b, wz, wsmall, wiwt, gq, gk, gik, tabs_p, mixer=j, tm=tm_p, kc=tq)
            kp_l.append(pr[1].reshape(bsz, s_len, N_KV_HEADS, LANES))
            vp_l.append(pr[2].reshape(bsz, s_len, N_KV_HEADS, LANES))
            ikp_l.append(pr[3].reshape(bsz, s_len, IDX_DIM))
            sr = _proj_b(ys, gn[i], win_b, wz, wsmall, wiwt, gq, gk, gik, tabs_s, mixer=j, tm=tm_s,
                         kc=min(tm_s, LANES))
            q_s, k_s, v_s, ik_s, _, _, ikbf_s, iq_s, iw_s, _, sz_s = sr
            ks_l.append(k_s.reshape(db, t_new, N_KV_HEADS, LANES))
            vs_l.append(v_s.reshape(db, t_new, N_KV_HEADS, LANES))
            iks_l.append(ik_s.reshape(db, t_new, IDX_DIM))
            bias = _idx_sample(page_table, iq_s, iw_s, ikbf_s, cik, layer=j, group=idx_group, t_new=t_new, topk=topk_s)

            dsa_args = dict(layer=i, mixer=j, bsz=bsz, s_len=s_len, tq=tq, topk=topk_p)
            prompt_steps = bsz * (s_len // tq)
            if db % prompt_steps == 0:
                yp, y_s = _dsa_prompt(pr, yp, pp, wout_b, gpe[i], wg, wpe, **dsa_args,
                                      sample=(page_table, q_s, bias, k_s, v_s, sz_s, ck, cv, j,
                                              db // prompt_steps, t_new))
            else:
                yp = _dsa_prompt(pr, yp, pp, wout_b, gpe[i], wg, wpe, **dsa_args)
                y_s = _attn_sample(page_table, q_s, bias, k_s, v_s, sz_s, ck, cv, layer=j, group=att_group, t_new=t_new)
            ys = _out_ple(y_s, ys, ps, wout_b, gpe[i], wg, wpe, layer=i, mixer=j, tm=tm_s)

    return (yp.reshape(bsz, s_len, d), ys.reshape(db, t_new, d),
            jnp.stack(kp_l, axis=1), jnp.stack(vp_l, axis=1), jnp.stack(ikp_l, axis=1),
            jnp.stack(ks_l, axis=1), jnp.stack(vs_l, axis=1), jnp.stack(iks_l, axis=1),
            jnp.stack(cv_l, axis=1))
```

```python
import functools

import jax
import jax.numpy as jnp
from jax import lax
from jax.experimental import pallas as pl
from jax.experimental.pallas import tpu as pltpu

EPS = 1e-6
ROPE_THETA = 10000.0
CHUNK = 128
A_GROUPS = 8
N_HEADS = 8
N_KV_HEADS = 2
IDX_HEADS = 8
IDX_DIM = 64
TOPK_MAX = 256
PAGE_SIZE = 128

LANES = 128
SUBLANES = 8
INT_MIN = -(2 ** 31)
F32_MAX_BITS = 0x7F7FFFFF
Q_SCALE = LANES ** -0.5 * 1.4426950408889634
NEG = -1e30
VMEM_LIMIT = 56 * 1024 * 1024

F32 = jnp.float32
BF16 = jnp.bfloat16
I32 = jnp.int32


def _bf(x):
    return x.astype(BF16)


def _dot(a, b):
    return jnp.dot(a, b, preferred_element_type=F32)


def _dot_nt(a, b):
    return lax.dot_general(a, b, (((1,), (1,)), ((), ())), preferred_element_type=F32)


def _rms(x, g):
    r = lax.rsqrt(jnp.mean(x * x, axis=-1, keepdims=True) + EPS)
    return x * r * g


def _silu(z):
    return z * jax.nn.sigmoid(z)


def _per_layer_embed(x1, p, gpe, wg, wpe):
    gate = jax.nn.sigmoid(_dot(_bf(_rms(x1, gpe)), wg))
    return x1 + gate * _dot(_bf(p), wpe)


def _key_to_f32(key):
    kc = jnp.clip(key, -F32_MAX_BITS, F32_MAX_BITS)
    return pltpu.bitcast(jnp.where(kc < 0, (-kc) | INT_MIN, kc), F32)


def _topk_threshold(count_ge, shape, kk):
    def one_pass(j, carry):
        key, cnt_key = carry
        cand = key + (jnp.int32(1) << (31 - j))
        cnt = count_ge(_key_to_f32(cand))
        ok = cnt >= kk
        return jnp.where(ok, cand, key), jnp.where(ok, cnt, cnt_key)

    key, cnt = lax.fori_loop(0, 32, one_pass, (jnp.full(shape, INT_MIN, I32), jnp.zeros(shape, F32)))
    return _key_to_f32(key), cnt


def _tie_cutoff(count_tie_before, shape, quota, nbits):
    def idx_pass(j, lim):
        cand = lim + (jnp.int32(1) << (nbits - 1 - j))
        return jnp.where(count_tie_before(cand) < quota, cand, lim)

    return lax.fori_loop(0, nbits, idx_pass, jnp.zeros(shape, I32))


def _const_spec(shape):
    nd = len(shape)
    return pl.BlockSpec(shape, lambda *_: (0,) * nd, pipeline_mode=pl.Buffered(1))


def _stacked_spec(w, index):
    return pl.BlockSpec((None,) + w.shape[1:], lambda *_: (index, 0, 0), pipeline_mode=pl.Buffered(1))


def _params(sem):
    return pltpu.CompilerParams(dimension_semantics=sem, vmem_limit_bytes=VMEM_LIMIT)


def _layer_a_kernel(x_ref, p_ref, gn_ref, win_ref, ws_ref, bs_ref, gv_ref, wout_ref,
                    gpe_ref, wg_ref, wpe_ref, xo_ref, *v_out, chunk):
    aw = gv_ref.shape[1]
    gd = aw // A_GROUPS
    x = x_ref[...]
    tm = x.shape[0]
    h = _bf(_rms(x, gn_ref[...]))
    u = _dot(h, win_ref[:, 0:aw])
    v = _rms(_dot(h, win_ref[:, aw:2 * aw]), gv_ref[...])
    z = _dot(h, win_ref[:, 2 * aw:3 * aw])
    if v_out:
        v_out[0][...] = v
    vb = _bf(v)
    row = lax.broadcasted_iota(I32, (CHUNK, CHUNK), 0)
    col = lax.broadcasted_iota(I32, (CHUNK, CHUNK), 1)
    keep = col <= row
    if chunk != CHUNK:
        keep = jnp.logical_and(keep, (row // chunk) == (col // chunk))
    cols = []
    for g in range(A_GROUPS):
        wsg = _bf(jnp.where(keep, ws_ref[g], 0.0))
        rows = [_dot(wsg, vb[j * CHUNK:(j + 1) * CHUNK, g * gd:(g + 1) * gd]) for j in range(tm // CHUNK)]
        cols.append(rows[0] if len(rows) == 1 else jnp.concatenate(rows, axis=0))
    mixed = jnp.concatenate(cols, axis=1)
    bias = bs_ref[...]
    if tm != CHUNK:
        bias = jnp.concatenate([bias] * (tm // CHUNK), axis=0)
    y = u * (mixed + bias) * _silu(z)
    x1 = x + _dot(_bf(y), wout_ref[...])
    xo_ref[...] = _per_layer_embed(x1, p_ref[...], gpe_ref[...], wg_ref[...], wpe_ref[...])


def _layer_p_spec(p, layer, rows, row_index):
    return pl.BlockSpec((None, rows, p.shape[2]), lambda *g: (layer, row_index(*g), 0))


def _layer_a(x, p, gn, win, ws_t, bs_t, gv, wout, gpe, wg, wpe, *, layer, mixer, chunk, tm, want_v):
    n, d = x.shape
    aw = gv.shape[1]
    pd = p.shape[2]
    row_spec = lambda w: pl.BlockSpec((tm, w), lambda i: (i, 0))
    out_shape = [jax.ShapeDtypeStruct((n, d), F32)]
    out_specs = [row_spec(d)]
    if want_v:
        out_shape.append(jax.ShapeDtypeStruct((n, aw), F32))
        out_specs.append(row_spec(aw))
    res = pl.pallas_call(
        functools.partial(_layer_a_kernel, chunk=chunk),
        grid=(n // tm,),
        in_specs=[row_spec(d), _layer_p_spec(p, layer, tm, lambda i: i), _const_spec((1, d)), _stacked_spec(win, mixer),
                  _const_spec((A_GROUPS, CHUNK, CHUNK)), _const_spec((CHUNK, aw)), _const_spec((1, aw)),
                  _stacked_spec(wout, mixer), _const_spec((1, d)), _stacked_spec(wg, layer), _stacked_spec(wpe, layer)],
        out_specs=out_specs,
        out_shape=out_shape,
        compiler_params=_params(("arbitrary",)),
        name="layer_a",
    )(x, p, gn, win, ws_t, bs_t, gv, wout, gpe, wg, wpe)
    return res if want_v else (res[0], None)


def _proj_b_kernel(x_ref, gn_ref, wqkv_ref, wiq_ref, wz_ref, wsmall_ref, wiwt_ref, gq_ref, gk_ref, gik_ref,
                   c128_ref, s128_ref, c64_ref, s64a_ref, s64b_ref,
                   q_ref, k_ref, v_ref, ik_ref, kbf_ref, vt_ref, ikbf_ref, iq_ref, iw_ref, iwt_ref, sz_ref, *, kc):
    dh = LANES
    bw = N_HEADS * dh
    kvw = N_KV_HEADS * dh
    iqw = IDX_HEADS * IDX_DIM
    x = x_ref[...]
    tm = x.shape[0]
    h = _bf(_rms(x, gn_ref[...]))
    small = _dot(h, wsmall_ref[...])
    iwt_ref[...] = _dot_nt(wiwt_ref[...], h)
    m_iq = _dot(h, wiq_ref[...])
    m_q = _dot(h, wqkv_ref[:, 0:bw])
    m_k = _dot(h, wqkv_ref[:, bw:bw + kvw])
    v = _dot(h, wqkv_ref[:, bw + kvw:bw + 2 * kvw])
    m_z = _dot(h, wz_ref[...])
    c128, s128 = c128_ref[...], s128_ref[...]
    c64, s64a, s64b = c64_ref[...], s64a_ref[...], s64b_ref[...]

    def rope128(t):
        return t * c128 + pltpu.roll(t, dh // 2, 1) * s128

    def rope64(t):
        return t * c64 + pltpu.roll(t, LANES - IDX_DIM // 2, 1) * s64a + pltpu.roll(t, IDX_DIM // 2, 1) * s64b

    lane = lax.broadcasted_iota(I32, (tm, LANES), 1)
    ms = jnp.sum(jnp.where(lane < IDX_DIM, small * small, 0.0), axis=-1, keepdims=True) * (1.0 / IDX_DIM)
    ikr = rope64(small * lax.rsqrt(ms + EPS) * gik_ref[...])[:, :IDX_DIM]
    ik_ref[...] = ikr
    ikbf_ref[...] = _bf(ikr)
    iw_ref[...] = small[:, IDX_DIM:IDX_DIM + IDX_HEADS]
    for j in range(iqw // LANES):
        iq_ref[:, j * LANES:(j + 1) * LANES] = _bf(rope64(m_iq[:, j * LANES:(j + 1) * LANES]))
    for hh in range(N_HEADS):
        qh = rope128(_rms(m_q[:, hh * dh:(hh + 1) * dh], gq_ref[...]))
        q_ref[:, hh * dh:(hh + 1) * dh] = _bf(qh * Q_SCALE)
    for hh in range(N_KV_HEADS):
        kh = rope128(_rms(m_k[:, hh * dh:(hh + 1) * dh], gk_ref[...]))
        k_ref[pl.ds(hh, tm, stride=N_KV_HEADS), :] = kh
        v_ref[pl.ds(hh, tm, stride=N_KV_HEADS), :] = v[:, hh * dh:(hh + 1) * dh]
        kbf_ref[:, hh * dh:(hh + 1) * dh] = _bf(kh)
    for j in range(tm // kc):
        vt_ref[j] = _bf(v[j * kc:(j + 1) * kc, :].T)
    sz_ref[...] = _silu(m_z)


def _proj_b(x, gn, win_b, wz, wsmall, wiwt, gq, gk, gik, tabs, *, mixer, tm, kc):
    n, d = x.shape
    bw = N_HEADS * LANES
    kvw = N_KV_HEADS * LANES
    iqw = IDX_HEADS * IDX_DIM
    qkvw = bw + 2 * kvw
    assert qkvw % iqw == 0
    period = tabs[0].shape[0] // tm
    row_spec = lambda w: pl.BlockSpec((tm, w), lambda i: (i, 0))
    tab_spec = pl.BlockSpec((tm, LANES), lambda i: (i % period, 0))
    sds = jax.ShapeDtypeStruct
    outs = [
        (sds((n, bw), BF16), row_spec(bw)),
        (sds((n * N_KV_HEADS, LANES), F32), pl.BlockSpec((tm * N_KV_HEADS, LANES), lambda i: (i, 0))),
        (sds((n * N_KV_HEADS, LANES), F32), pl.BlockSpec((tm * N_KV_HEADS, LANES), lambda i: (i, 0))),
        (sds((n, IDX_DIM), F32), row_spec(IDX_DIM)),
        (sds((n, kvw), BF16), row_spec(kvw)),
        (sds((n // kc, kvw, kc), BF16), pl.BlockSpec((tm // kc, kvw, kc), lambda i: (i, 0, 0))),
        (sds((n, IDX_DIM), BF16), row_spec(IDX_DIM)),
        (sds((n, iqw), BF16), row_spec(iqw)),
        (sds((n, IDX_HEADS), F32), row_spec(IDX_HEADS)),
        (sds((IDX_HEADS, n), F32), pl.BlockSpec((IDX_HEADS, tm), lambda i: (0, i))),
        (sds((n, bw), F32), row_spec(bw)),
    ]
    return pl.pallas_call(
        functools.partial(_proj_b_kernel, kc=kc),
        grid=(n // tm,),
        in_specs=[row_spec(d), _const_spec((1, d)),
                  pl.BlockSpec((None, d, qkvw), lambda i: (mixer, 0, 0), pipeline_mode=pl.Buffered(1)),
                  pl.BlockSpec((None, d, iqw), lambda i: (mixer, 0, qkvw // iqw), pipeline_mode=pl.Buffered(1)),
                  _const_spec(wz.shape), _const_spec(wsmall.shape),
                  _const_spec(wiwt.shape), _const_spec((1, LANES)), _const_spec((1, LANES)), _const_spec((1, LANES)),
                  tab_spec, tab_spec, tab_spec, tab_spec, tab_spec],
        out_specs=[o[1] for o in outs],
        out_shape=[o[0] for o in outs],
        compiler_params=_params(("arbitrary",)),
        name="proj_b",
    )(x, gn, win_b, win_b, wz, wsmall, wiwt, gq, gk, gik, *tabs)


def _dsa_prompt_sample_kernel(pt_ref, *refs, topk, sample):
    prompt_in, sample_in = refs[:13], refs[13:20]
    xo_ref, y_ref = refs[20:22]
    prompt_scratch, sample_scratch = refs[22:26], refs[26:30]
    step = pl.program_id(0) * pl.num_programs(1) + pl.program_id(1)
    nsteps = pl.num_programs(0) * pl.num_programs(1)
    _sample_attention_step(pt_ref, *sample_in, y_ref, *sample_scratch, step, nsteps, **sample)
    _dsa_prompt_kernel(*prompt_in, xo_ref, *prompt_scratch, topk=topk)


def _dsa_prompt_kernel(iq_ref, iwt_ref, ikbf_ref, kbf_ref, vt_ref, q_ref, sz_ref, x_ref, p_ref,
                       wout_ref, gpe_ref, wg_ref, wpe_ref, xo_ref, sc_ref, m_sc, l_sc, acc_sc, *, topk):
    tq = q_ref.shape[0]
    s_len = sc_ref.shape[0]
    dh = LANES
    i = pl.program_id(1)
    nch = i + 1
    qpos = i * tq + lax.broadcasted_iota(I32, (1, tq), 1)
    krow = lax.broadcasted_iota(I32, (tq, tq), 0)
    idx_scale = IDX_DIM ** -0.5 * IDX_HEADS ** -0.5

    def chunk_start(c):
        return pl.multiple_of(c * tq, tq)

    def idx_dots(c):
        ikc = ikbf_ref[pl.ds(chunk_start(c), tq), :]
        return [_dot_nt(ikc, iq_ref[:, h * IDX_DIM:(h + 1) * IDX_DIM]) for h in range(IDX_HEADS)]

    def idx_combine(c, xs):
        off = chunk_start(c)
        s = jnp.maximum(xs[0], 0.0) * iwt_ref[0:1, :]
        for h in range(1, IDX_HEADS):
            s = s + jnp.maximum(xs[h], 0.0) * iwt_ref[h:h + 1, :]
        sc_ref[pl.ds(off, tq), :] = jnp.where(off + krow <= qpos, s * idx_scale, -jnp.inf)

    def build_pair(j, carry):
        xs0 = idx_dots(2 * j)
        xs1 = idx_dots(2 * j + 1)
        idx_combine(2 * j, xs0)
        idx_combine(2 * j + 1, xs1)
        return carry

    lax.fori_loop(0, nch // 2, build_pair, 0)

    @pl.when(nch % 2 == 1)
    def _():
        idx_combine(nch - 1, idx_dots(nch - 1))

    def count(pred):
        def body(c, acc):
            off = chunk_start(c)
            m = jnp.where(pred(sc_ref[pl.ds(off, tq), :], off + krow), 1.0, 0.0)
            part = m[0:32]
            for r in range(1, tq // 32):
                part = part + m[r * 32:(r + 1) * 32]
            return acc + part
        acc = lax.fori_loop(0, nch, body, jnp.zeros((32, tq), F32))
        return jnp.sum(acc, axis=0, keepdims=True)

    kk = float(topk)
    t_thr, cnt_thr = _topk_threshold(lambda t: count(lambda blk, pos: blk >= t), (1, tq), kk)

    need = cnt_thr > kk

    @pl.when(jnp.max(jnp.where(need, 1.0, 0.0)) > 0.0)
    def _():
        quota = kk - count(lambda blk, pos: blk > t_thr)
        last = _tie_cutoff(lambda cand: count(lambda blk, pos: jnp.logical_and(blk == t_thr, pos < cand)),
                           (1, tq), quota, max(1, (s_len - 1).bit_length()))

        def demote(c, carry):
            off = chunk_start(c)
            blk = sc_ref[pl.ds(off, tq), :]
            drop = jnp.logical_and(jnp.logical_and(blk == t_thr, off + krow > last), need)
            sc_ref[pl.ds(off, tq), :] = jnp.where(drop, -jnp.inf, blk)
            return carry

        lax.fori_loop(0, nch, demote, 0)

    m_sc[...] = jnp.full(m_sc.shape, NEG, F32)
    l_sc[...] = jnp.zeros(l_sc.shape, F32)
    acc_sc[...] = jnp.zeros(acc_sc.shape, F32)
    rep = N_HEADS // N_KV_HEADS
    sub = SUBLANES

    def qk_dots(c):
        kc_all = kbf_ref[pl.ds(chunk_start(c), tq), :]
        return [_dot_nt(kc_all[:, (h // rep) * dh:(h // rep + 1) * dh], q_ref[:, h * dh:(h + 1) * dh])
                for h in range(N_HEADS)]

    def softmax_pv(c, qk):
        bias = jnp.where(sc_ref[pl.ds(chunk_start(c), tq), :] >= t_thr, 0.0, NEG)
        for g in range(N_KV_HEADS):
            vg = vt_ref[c, g * dh:(g + 1) * dh, :]
            for r in range(rep):
                h = g * rep + r
                st = slice(h * sub, (h + 1) * sub)
                s = (bias + qk[h]).reshape(tq // sub, sub, tq)
                m_old = m_sc[st, :]
                m_new = jnp.maximum(m_old, jnp.max(jnp.max(s, axis=0), axis=0, keepdims=True))
                p = jnp.exp2(s - m_new[None])
                alpha = jnp.exp2(m_old - m_new)
                l_sc[st, :] = alpha * l_sc[st, :] + jnp.sum(p, axis=0)
                acc_sc[h * dh:(h + 1) * dh, :] = (alpha[0:1] * acc_sc[h * dh:(h + 1) * dh, :]
                                                  + _dot(vg, _bf(p.reshape(tq, tq))))
                m_sc[st, :] = m_new

    def attend_pair(j, carry):
        qk0 = qk_dots(2 * j)
        qk1 = qk_dots(2 * j + 1)
        softmax_pv(2 * j, qk0)
        softmax_pv(2 * j + 1, qk1)
        return carry

    lax.fori_loop(0, nch // 2, attend_pair, 0)

    @pl.when(nch % 2 == 1)
    def _():
        softmax_pv(nch - 1, qk_dots(nch - 1))

    inv = [1.0 / jnp.sum(l_sc[h * sub:(h + 1) * sub, :], axis=0, keepdims=True) for h in range(N_HEADS)]
    att_t = jnp.concatenate([acc_sc[h * dh:(h + 1) * dh, :] * inv[h] for h in range(N_HEADS)], axis=0)
    y = _bf(att_t.T * sz_ref[...])
    x1 = x_ref[...] + _dot(y, wout_ref[...])
    xo_ref[...] = _per_layer_embed(x1, p_ref[...], gpe_ref[...], wg_ref[...], wpe_ref[...])


def _dsa_prompt(pr, x, p, wout, gpe, wg, wpe, *, layer, mixer, bsz, s_len, tq, topk, sample=None):
    q, _, _, _, kbf, vt, ikbf, iq, _, iwt, sz = pr
    n, d = x.shape
    bw = q.shape[1]
    kvw = kbf.shape[1]
    nq = s_len // tq
    blk = lambda w: pl.BlockSpec((tq, w), lambda b, i, *_: (b * nq + i, 0))
    per_b = lambda w: pl.BlockSpec((s_len, w), lambda b, i, *_: (b, 0))
    in_specs = [blk(iq.shape[1]),
                pl.BlockSpec((IDX_HEADS, tq), lambda b, i, *_: (0, b * nq + i)),
                per_b(IDX_DIM), per_b(kvw),
                pl.BlockSpec((nq, kvw, tq), lambda b, i, *_: (b, 0, 0)),
                blk(bw), blk(bw), blk(d), _layer_p_spec(p, layer, tq, lambda b, i, *_: b * nq + i),
                _stacked_spec(wout, mixer), _const_spec((1, d)), _stacked_spec(wg, layer), _stacked_spec(wpe, layer)]
    scratch = [pltpu.VMEM((s_len, tq), F32), pltpu.VMEM((N_HEADS * SUBLANES, tq), F32),
               pltpu.VMEM((N_HEADS * SUBLANES, tq), F32), pltpu.VMEM((bw, tq), F32)]
    args = (iq, iwt, ikbf, kbf, vt, q, sz, x, p, wout, gpe, wg, wpe)
    if sample is None:
        return pl.pallas_call(
            functools.partial(_dsa_prompt_kernel, topk=topk),
            grid=(bsz, nq), in_specs=in_specs, out_specs=blk(d),
            out_shape=jax.ShapeDtypeStruct((n, d), F32), scratch_shapes=scratch,
            compiler_params=_params(("arbitrary", "arbitrary")), name="dsa_prompt",
        )(*args)

    page_table, q_s, bias, k_new, v_new, sz_s, cache_k, cache_v, cache_layer, group, t_new = sample
    n_pages = page_table.shape[1]
    lp = n_pages * PAGE_SIZE + LANES
    rows = group * t_new
    sblk = lambda r, w: pl.BlockSpec((r, w), lambda b, i, *_: (b * nq + i, 0))
    any_spec = pl.BlockSpec(memory_space=pl.ANY)
    return pl.pallas_call(
        functools.partial(_dsa_prompt_sample_kernel, topk=topk,
                          sample=dict(group=group, n_pages=n_pages, layer=cache_layer, t_new=t_new)),
        grid_spec=pltpu.PrefetchScalarGridSpec(
            num_scalar_prefetch=1,
            grid=(bsz, nq),
            in_specs=in_specs + [sblk(rows, bw), sblk(rows, lp), sblk(rows * N_KV_HEADS, LANES),
                                 sblk(rows * N_KV_HEADS, LANES), sblk(rows, bw), any_spec, any_spec],
            out_specs=[blk(d), sblk(rows, bw)],
            scratch_shapes=scratch + [pltpu.VMEM((2, group, lp * N_KV_HEADS, LANES), F32),
                                      pltpu.VMEM((2, group, lp * N_KV_HEADS, LANES), F32),
                                      pltpu.SemaphoreType.DMA((2,)), pltpu.SemaphoreType.DMA((2,))]),
        out_shape=[jax.ShapeDtypeStruct((n, d), F32), jax.ShapeDtypeStruct(q_s.shape, F32)],
        compiler_params=_params(("arbitrary", "arbitrary")),
        name="dsa_prompt_sample",
    )(page_table, *args, q_s, bias, k_new, v_new, sz_s, cache_k, cache_v)


def _page_copies(pt_ref, cache_ref, buf, sem, step, slot, *, group, n_pages, layer, rows_per_page, axis):
    copies = []
    for b in range(group):
        for n in range(n_pages):
            page = pt_ref[step * group + b, n]
            span = pl.ds(n * rows_per_page, rows_per_page)
            dst = buf.at[slot, b, span, :] if axis == 0 else buf.at[slot, b, :, span]
            copies.append(pltpu.make_async_copy(cache_ref.at[page, layer], dst, sem.at[slot]))
    return copies


def _paged_pipeline(make_copies_list, step, nsteps):
    slot = step % 2

    @pl.when(step == 0)
    def _():
        for mk in make_copies_list:
            for cp in mk(step, slot):
                cp.start()

    @pl.when(step + 1 < nsteps)
    def _():
        for mk in make_copies_list:
            for cp in mk(step + 1, 1 - slot):
                cp.start()

    for mk in make_copies_list:
        for cp in mk(step, slot):
            cp.wait()
    return slot


def _idx_sample_kernel(pt_ref, iq_ref, iw_ref, iknew_ref, cik_ref, bias_ref, ikbuf, sem, sc_ref,
                       *, group, n_pages, layer, t_new, topk):
    past = n_pages * PAGE_SIZE
    lp = past + LANES
    slot = _paged_pipeline([functools.partial(_page_copies, pt_ref, cik_ref, ikbuf, sem, group=group,
                                              n_pages=n_pages, layer=layer, rows_per_page=PAGE_SIZE, axis=1)],
                           pl.program_id(0), pl.num_programs(0))
    idx_scale = IDX_DIM ** -0.5 * IDX_HEADS ** -0.5
    iq_all = iq_ref[...].astype(F32)
    iknew_all = iknew_ref[...].astype(F32)
    iw_all = iw_ref[...]
    kpos = lax.broadcasted_iota(I32, (t_new, lp), 1)
    qpos = past + lax.broadcasted_iota(I32, (t_new, lp), 0)
    dots = []
    for b in range(group):
        rows = slice(b * t_new, (b + 1) * t_new)
        iqb = iq_all[rows]
        iqs = _bf(jnp.concatenate([iqb[:, h * IDX_DIM:(h + 1) * IDX_DIM] for h in range(IDX_HEADS)], axis=0))
        ik_new = _bf(jnp.concatenate([iknew_all[rows], jnp.zeros((LANES - t_new, IDX_DIM), F32)], axis=0))
        dots.append((_dot(iqs, _bf(ikbuf[slot, b])), _dot_nt(iqs, ik_new)))
    for b in range(group):
        rows = slice(b * t_new, (b + 1) * t_new)
        xp, xn = dots[b]
        iwb = iw_all[rows]
        sp = jnp.zeros((t_new, past), F32)
        sn = jnp.zeros((t_new, LANES), F32)
        for h in range(IDX_HEADS):
            w = iwb[:, h:h + 1]
            sp = sp + jnp.maximum(xp[h * t_new:(h + 1) * t_new], 0.0) * w
            sn = sn + jnp.maximum(xn[h * t_new:(h + 1) * t_new], 0.0) * w
        sc = jnp.concatenate([sp, sn], axis=1) * idx_scale
        sc_ref[rows, :] = jnp.where(kpos <= qpos, sc, -jnp.inf)

    kk = float(topk)
    rows_all = group * t_new
    kpos_all = lax.broadcasted_iota(I32, (rows_all, lp), 1)

    def count(pred):
        return jnp.sum(jnp.where(pred(sc_ref[...]), 1.0, 0.0), axis=1, keepdims=True)

    t_thr, cnt_thr = _topk_threshold(lambda t: count(lambda blk: blk >= t), (rows_all, 1), kk)
    need = cnt_thr > kk

    @pl.when(jnp.max(jnp.where(need, 1.0, 0.0)) > 0.0)
    def _():
        quota = kk - count(lambda blk: blk > t_thr)
        last = _tie_cutoff(lambda cand: count(lambda blk: jnp.logical_and(blk == t_thr, kpos_all < cand)),
                           (rows_all, 1), quota, max(1, (lp - 1).bit_length()))
        blk = sc_ref[...]
        drop = jnp.logical_and(jnp.logical_and(blk == t_thr, kpos_all > last), need)
        sc_ref[...] = jnp.where(drop, -jnp.inf, blk)

    bias_ref[...] = jnp.where(sc_ref[...] >= t_thr, 0.0, NEG)


def _idx_sample(page_table, iq, iw, ikbf, cache_ik, *, layer, group, t_new, topk):
    n = iq.shape[0]
    db, n_pages = page_table.shape
    past = n_pages * PAGE_SIZE
    lp = past + LANES
    rows = group * t_new
    blk = lambda w: pl.BlockSpec((rows, w), lambda i, pt: (i, 0))
    return pl.pallas_call(
        functools.partial(_idx_sample_kernel, group=group, n_pages=n_pages, layer=layer, t_new=t_new, topk=topk),
        grid_spec=pltpu.PrefetchScalarGridSpec(
            num_scalar_prefetch=1,
            grid=(db // group,),
            in_specs=[blk(iq.shape[1]), blk(IDX_HEADS), blk(IDX_DIM), pl.BlockSpec(memory_space=pl.ANY)],
            out_specs=blk(lp),
            scratch_shapes=[pltpu.VMEM((2, group, IDX_DIM, past), F32), pltpu.SemaphoreType.DMA((2,)),
                            pltpu.VMEM((rows, lp), F32)]),
        out_shape=jax.ShapeDtypeStruct((n, lp), F32),
        compiler_params=_params(("arbitrary",)),
        name="idx_sample",
    )(page_table, iq, iw, ikbf, cache_ik)


def _attn_sample_kernel(pt_ref, q_ref, bias_ref, knew_ref, vnew_ref, sz_ref, ck_ref, cv_ref, y_ref,
                        kbuf, vbuf, ksem, vsem, **static):
    _sample_attention_step(pt_ref, q_ref, bias_ref, knew_ref, vnew_ref, sz_ref, ck_ref, cv_ref, y_ref,
                           kbuf, vbuf, ksem, vsem, pl.program_id(0), pl.num_programs(0), **static)


def _sample_attention_step(pt_ref, q_ref, bias_ref, knew_ref, vnew_ref, sz_ref, ck_ref, cv_ref, y_ref,
                           kbuf, vbuf, ksem, vsem, step, nsteps, *, group, n_pages, layer, t_new):
    past = n_pages * PAGE_SIZE
    lp = past + LANES
    dh = LANES
    nkv = N_KV_HEADS
    rep = N_HEADS // nkv
    copy_args = dict(group=group, n_pages=n_pages, layer=layer, rows_per_page=PAGE_SIZE * nkv, axis=0)
    slot = _paged_pipeline([functools.partial(_page_copies, pt_ref, ck_ref, kbuf, ksem, **copy_args),
                            functools.partial(_page_copies, pt_ref, cv_ref, vbuf, vsem, **copy_args)],
                           step, nsteps)
    q_all = q_ref[...].astype(F32)
    pad = jnp.zeros(((LANES - t_new) * nkv, dh), F32)
    qk = {}
    for b in range(group):
        rows = slice(b * t_new, (b + 1) * t_new)
        new_rows = slice(b * t_new * nkv, (b + 1) * t_new * nkv)
        tail = pl.ds(past * nkv, LANES * nkv)
        kbuf[slot, b, tail, :] = jnp.concatenate([knew_ref[new_rows, :], pad], axis=0)
        vbuf[slot, b, tail, :] = jnp.concatenate([vnew_ref[new_rows, :], pad], axis=0)
        qb = q_all[rows]
        for g in range(nkv):
            kg = _bf(kbuf[slot, b, pl.ds(g, lp, stride=nkv), :])
            qg = _bf(jnp.concatenate([qb[:, (g * rep + r) * dh:(g * rep + r + 1) * dh] for r in range(rep)], axis=0))
            qk[b, g] = _dot_nt(qg, kg)
    for b in range(group):
        rows = slice(b * t_new, (b + 1) * t_new)
        bias = jnp.concatenate([bias_ref[rows, :]] * rep, axis=0)
        outs = []
        for g in range(nkv):
            vg = _bf(vbuf[slot, b, pl.ds(g, lp, stride=nkv), :])
            s = bias + qk[b, g]
            m = jnp.max(s, axis=1, keepdims=True)
            p = jnp.exp2(s - m)
            l = jnp.sum(p, axis=1, keepdims=True)
            o = _dot(_bf(p), vg) * (1.0 / l)
            outs.extend(o[r * t_new:(r + 1) * t_new] for r in range(rep))
        y_ref[rows, :] = jnp.concatenate(outs, axis=1) * sz_ref[rows, :]


def _attn_sample(page_table, q, bias, knew, vnew, sz, cache_k, cache_v, *, layer, group, t_new):
    n, bw = q.shape
    db, n_pages = page_table.shape
    past = n_pages * PAGE_SIZE
    lp = past + LANES
    rows = group * t_new
    blk = lambda w: pl.BlockSpec((rows, w), lambda i, pt: (i, 0))
    new_blk = pl.BlockSpec((rows * N_KV_HEADS, LANES), lambda i, pt: (i, 0))
    return pl.pallas_call(
        functools.partial(_attn_sample_kernel, group=group, n_pages=n_pages, layer=layer, t_new=t_new),
        grid_spec=pltpu.PrefetchScalarGridSpec(
            num_scalar_prefetch=1,
            grid=(db // group,),
            in_specs=[blk(bw), blk(lp), new_blk, new_blk, blk(bw),
                      pl.BlockSpec(memory_space=pl.ANY), pl.BlockSpec(memory_space=pl.ANY)],
            out_specs=blk(bw),
            scratch_shapes=[pltpu.VMEM((2, group, lp * N_KV_HEADS, LANES), F32),
                            pltpu.VMEM((2, group, lp * N_KV_HEADS, LANES), F32),
                            pltpu.SemaphoreType.DMA((2,)), pltpu.SemaphoreType.DMA((2,))]),
        out_shape=jax.ShapeDtypeStruct((n, bw), F32),
        compiler_params=_params(("arbitrary",)),
        name="attn_sample",
    )(page_table, q, bias, knew, vnew, sz, cache_k, cache_v)


def _out_ple_kernel(y_ref, x_ref, p_ref, wout_ref, gpe_ref, wg_ref, wpe_ref, xo_ref):
    x1 = x_ref[...] + _dot(_bf(y_ref[...]), wout_ref[...])
    xo_ref[...] = _per_layer_embed(x1, p_ref[...], gpe_ref[...], wg_ref[...], wpe_ref[...])


def _out_ple(y, x, p, wout, gpe, wg, wpe, *, layer, mixer, tm):
    n, d = x.shape
    row_spec = lambda w: pl.BlockSpec((tm, w), lambda i: (i, 0))
    return pl.pallas_call(
        _out_ple_kernel,
        grid=(n // tm,),
        in_specs=[row_spec(y.shape[1]), row_spec(d), _layer_p_spec(p, layer, tm, lambda i: i),
                  _stacked_spec(wout, mixer), _const_spec((1, d)), _stacked_spec(wg, layer), _stacked_spec(wpe, layer)],
        out_specs=row_spec(d),
        out_shape=jax.ShapeDtypeStruct((n, d), F32),
        compiler_params=_params(("arbitrary",)),
        name="out_ple",
    )(y, x, p, wout, gpe, wg, wpe)


def _rope_tables(pos):
    def angles(half):
        inv = ROPE_THETA ** (-jnp.arange(half, dtype=F32) / half)
        return pos.astype(F32)[:, None] * inv[None, :]
    a = angles(LANES // 2)
    c128 = jnp.concatenate([jnp.cos(a), jnp.cos(a)], axis=1)
    s128 = jnp.concatenate([-jnp.sin(a), jnp.sin(a)], axis=1)
    a = angles(IDX_DIM // 2)
    zero = jnp.zeros_like(a)
    c64 = jnp.tile(jnp.cos(a), (1, 4))
    s64a = jnp.tile(jnp.concatenate([-jnp.sin(a), zero], axis=1), (1, 2))
    s64b = jnp.tile(jnp.concatenate([zero, jnp.sin(a)], axis=1), (1, 2))
    return c128, s128, c64, s64a, s64b


def _tile_rows(n, pref):
    t = min(n, pref)
    assert n % t == 0, (n, t)
    return t


def kernel(x_prompt, x_sample, cache_k, cache_v, cache_ik, page_table, p_prompt, p_sample, g_norm, w_in_a, w_s_a, b_s_a, g_v_a, w_out_a, w_in_b, g_q_b, g_k_b, g_ik_b, w_out_b, g_pe, w_pe_gate, w_pe_proj):
    bsz, s_len, d = x_prompt.shape
    db, t_new, _ = x_sample.shape
    depth = g_norm.shape[0]
    n_pool, nb_layers = cache_k.shape[0], cache_k.shape[1]
    n_pages = page_table.shape[1]
    past = n_pages * PAGE_SIZE
    aw = g_v_a.shape[1]
    bw = N_HEADS * LANES
    kvw = N_KV_HEADS * LANES
    iqw = IDX_HEADS * IDX_DIM
    assert d == bw and s_len % CHUNK == 0 and CHUNK % t_new == 0 and t_new % 8 == 0
    assert w_in_b.shape[2] == 2 * bw + 2 * kvw + iqw + IDX_DIM + IDX_HEADS

    n_p, n_s = bsz * s_len, db * t_new
    yp = x_prompt.reshape(n_p, d)
    ys = x_sample.reshape(n_s, d)
    pp = p_prompt.reshape(depth, n_p, -1)
    ps = p_sample.reshape(depth, n_s, -1)
    ck = cache_k.reshape(n_pool, nb_layers, PAGE_SIZE * N_KV_HEADS, LANES)
    cv = cache_v.reshape(n_pool, nb_layers, PAGE_SIZE * N_KV_HEADS, LANES)
    cik = jnp.swapaxes(cache_ik, 2, 3)

    tm_p = _tile_rows(n_p, 256)
    tm_s = _tile_rows(n_s, 256)
    tq = _tile_rows(s_len, 256)
    topk_p = min(TOPK_MAX, s_len // 4)
    topk_s = min(TOPK_MAX, (past + t_new) // 4)
    idx_group = _tile_rows(db, 32)
    att_group = _tile_rows(db, 2)

    tabs_p = _rope_tables(jnp.arange(s_len))
    tabs_s = _rope_tables(past + (jnp.arange(tm_s) % t_new))

    gn = g_norm.reshape(depth, 1, d)
    gpe = g_pe.reshape(depth, 1, d)
    wg = _bf(w_pe_gate)
    wpe = _bf(w_pe_proj)
    win_a = _bf(w_in_a)
    wout_a = _bf(w_out_a)
    wout_b = _bf(w_out_b)
    win_b = _bf(w_in_b)

    kp_l, vp_l, ikp_l, ks_l, vs_l, iks_l, cv_l = [], [], [], [], [], [], []
    for i in range(depth):
        j = i // 2
        if i % 2 == 0:
            gv = g_v_a[j].reshape(1, aw)
            reps = CHUNK // t_new
            ws_p = w_s_a[j]
            bs_p = jnp.repeat(b_s_a[j].T, aw // A_GROUPS, axis=1)
            ws_s = jnp.tile(w_s_a[j][:, :t_new, :t_new], (1, reps, reps))
            bs_s = jnp.tile(jnp.repeat(b_s_a[j][:, :t_new].T, aw // A_GROUPS, axis=1), (reps, 1))
            yp, _ = _layer_a(yp, pp, gn[i], win_a, ws_p, bs_p, gv, wout_a, gpe[i], wg, wpe,
                             layer=i, mixer=j, chunk=CHUNK, tm=tm_p, want_v=False)
            ys, v_rows = _layer_a(ys, ps, gn[i], win_a, ws_s, bs_s, gv, wout_a, gpe[i], wg, wpe,
                                  layer=i, mixer=j, chunk=t_new, tm=tm_s, want_v=True)
            cv_l.append(v_rows.reshape(db, t_new, aw))
        else:
            w = w_in_b[j]
            o_iq = bw + 2 * kvw
            o_ik = o_iq + iqw
            o_iw = o_ik + IDX_DIM
            o_z = o_iw + IDX_HEADS
            wz = _bf(w[:, o_z:])
            wsmall = _bf(jnp.pad(w[:, o_ik:o_z], ((0, 0), (0, LANES - IDX_DIM - IDX_HEADS))))
            wiwt = _bf(w[:, o_iw:o_z].T)
            gq = g_q_b[j].reshape(1, LANES)
            gk = g_k_b[j].reshape(1, LANES)
            gik = jnp.pad(g_ik_b[j], (0, LANES - IDX_DIM)).reshape(1, LANES)

            pr = _proj_b(yp, gn[i], win_b, wz, wsmall, wiwt, gq, gk, gik, tabs_p, mixer=j, tm=tm_p, kc=tq)
            kp_l.append(pr[1].reshape(bsz, s_len, N_KV_HEADS, LANES))
            vp_l.append(pr[2].reshape(bsz, s_len, N_KV_HEADS, LANES))
            ikp_l.append(pr[3].reshape(bsz, s_len, IDX_DIM))
            sr = _proj_b(ys, gn[i], win_b, wz, wsmall, wiwt, gq, gk, gik, tabs_s, mixer=j, tm=tm_s,
                         kc=min(tm_s, LANES))
            q_s, k_s, v_s, ik_s, _, _, ikbf_s, iq_s, iw_s, _, sz_s = sr
            ks_l.append(k_s.reshape(db, t_new, N_KV_HEADS, LANES))
            vs_l.append(v_s.reshape(db, t_new, N_KV_HEADS, LANES))
            iks_l.append(ik_s.reshape(db, t_new, IDX_DIM))
            bias = _idx_sample(page_table, iq_s, iw_s, ikbf_s, cik, layer=j, group=idx_group, t_new=t_new, topk=topk_s)

            dsa_args = dict(layer=i, mixer=j, bsz=bsz, s_len=s_len, tq=tq, topk=topk_p)
            prompt_steps = bsz * (s_len // tq)
            if db % prompt_steps == 0:
                yp, y_s = _dsa_prompt(pr, yp, pp, wout_b, gpe[i], wg, wpe, **dsa_args,
                                      sample=(page_table, q_s, bias, k_s, v_s, sz_s, ck, cv, j,
                                              db // prompt_steps, t_new))
            else:
                yp = _dsa_prompt(pr, yp, pp, wout_b, gpe[i], wg, wpe, **dsa_args)
                y_s = _attn_sample(page_table, q_s, bias, k_s, v_s, sz_s, ck, cv, layer=j, group=att_group, t_new=t_new)
            ys = _out_ple(y_s, ys, ps, wout_b, gpe[i], wg, wpe, layer=i, mixer=j, tm=tm_s)

    return (yp.reshape(bsz, s_len, d), ys.reshape(db, t_new, d),
            jnp.stack(kp_l, axis=1), jnp.stack(vp_l, axis=1), jnp.stack(ikp_l, axis=1),
            jnp.stack(ks_l, axis=1), jnp.stack(vs_l, axis=1), jnp.stack(iks_l, axis=1),
            jnp.stack(cv_l, axis=1))
```

```python
import functools

import jax
import jax.numpy as jnp
from jax import lax
from jax.experimental import pallas as pl
from jax.experimental.pallas import tpu as pltpu

EPS = 1e-6
ROPE_THETA = 10000.0
CHUNK = 128
A_GROUPS = 8
N_HEADS = 8
N_KV_HEADS = 2
IDX_HEADS = 8
IDX_DIM = 64
TOPK_MAX = 256
PAGE_SIZE = 128

LANES = 128
SUBLANES = 8
INT_MIN = -(2 ** 31)
F32_MAX_BITS = 0x7F7FFFFF
Q_SCALE = LANES ** -0.5 * 1.4426950408889634
NEG = -1e30
VMEM_LIMIT = 56 * 1024 * 1024

F32 = jnp.float32
BF16 = jnp.bfloat16
I32 = jnp.int32


def _bf(x):
    return x.astype(BF16)


def _dot(a, b):
    return jnp.dot(a, b, preferred_element_type=F32)


def _dot_nt(a, b):
    return lax.dot_general(a, b, (((1,), (1,)), ((), ())), preferred_element_type=F32)


def _rms(x, g):
    r = lax.rsqrt(jnp.mean(x * x, axis=-1, keepdims=True) + EPS)
    return x * r * g


def _silu(z):
    return z * jax.nn.sigmoid(z)


def _per_layer_embed(x1, p, gpe, wg, wpe):
    gate = jax.nn.sigmoid(_dot(_bf(_rms(x1, gpe)), wg))
    return x1 + gate * _dot(_bf(p), wpe)


def _key_to_f32(key):
    kc = jnp.clip(key, -F32_MAX_BITS, F32_MAX_BITS)
    return pltpu.bitcast(jnp.where(kc < 0, (-kc) | INT_MIN, kc), F32)


def _topk_threshold(count_ge, shape, kk):
    def one_pass(j, carry):
        key, cnt_key = carry
        cand = key + (jnp.int32(1) << (31 - j))
        cnt = count_ge(_key_to_f32(cand))
        ok = cnt >= kk
        return jnp.where(ok, cand, key), jnp.where(ok, cnt, cnt_key)

    key, cnt = lax.fori_loop(0, 32, one_pass, (jnp.full(shape, INT_MIN, I32), jnp.zeros(shape, F32)))
    return _key_to_f32(key), cnt


def _tie_cutoff(count_tie_before, shape, quota, nbits):
    def idx_pass(j, lim):
        cand = lim + (jnp.int32(1) << (nbits - 1 - j))
        return jnp.where(count_tie_before(cand) < quota, cand, lim)

    return lax.fori_loop(0, nbits, idx_pass, jnp.zeros(shape, I32))


def _const_spec(shape):
    nd = len(shape)
    return pl.BlockSpec(shape, lambda *_: (0,) * nd, pipeline_mode=pl.Buffered(1))


def _stacked_spec(w, index):
    return pl.BlockSpec((None,) + w.shape[1:], lambda *_: (index, 0, 0), pipeline_mode=pl.Buffered(1))


def _params(sem):
    return pltpu.CompilerParams(dimension_semantics=sem, vmem_limit_bytes=VMEM_LIMIT)


def _layer_a_kernel(x_ref, p_ref, gn_ref, win_ref, ws_ref, bs_ref, gv_ref, wout_ref,
                    gpe_ref, wg_ref, wpe_ref, xo_ref, *v_out, chunk):
    aw = gv_ref.shape[1]
    gd = aw // A_GROUPS
    x = x_ref[...]
    tm = x.shape[0]
    h = _bf(_rms(x, gn_ref[...]))
    u = _dot(h, win_ref[:, 0:aw])
    v = _rms(_dot(h, win_ref[:, aw:2 * aw]), gv_ref[...])
    z = _dot(h, win_ref[:, 2 * aw:3 * aw])
    if v_out:
        v_out[0][...] = v
    vb = _bf(v)
    row = lax.broadcasted_iota(I32, (CHUNK, CHUNK), 0)
    col = lax.broadcasted_iota(I32, (CHUNK, CHUNK), 1)
    keep = col <= row
    if chunk != CHUNK:
        keep = jnp.logical_and(keep, (row // chunk) == (col // chunk))
    cols = []
    for g in range(A_GROUPS):
        wsg = _bf(jnp.where(keep, ws_ref[g], 0.0))
        rows = [_dot(wsg, vb[j * CHUNK:(j + 1) * CHUNK, g * gd:(g + 1) * gd]) for j in range(tm // CHUNK)]
        cols.append(rows[0] if len(rows) == 1 else jnp.concatenate(rows, axis=0))
    mixed = jnp.concatenate(cols, axis=1)
    bias = bs_ref[...]
    if tm != CHUNK:
        bias = jnp.concatenate([bias] * (tm // CHUNK), axis=0)
    y = u * (mixed + bias) * _silu(z)
    x1 = x + _dot(_bf(y), wout_ref[...])
    xo_ref[...] = _per_layer_embed(x1, p_ref[...], gpe_ref[...], wg_ref[...], wpe_ref[...])


def _layer_p_spec(p, layer, rows, row_index):
    return pl.BlockSpec((None, rows, p.shape[2]), lambda *g: (layer, row_index(*g), 0))


def _layer_a(x, p, gn, win, ws_t, bs_t, gv, wout, gpe, wg, wpe, *, layer, mixer, chunk, tm, want_v):
    n, d = x.shape
    aw = gv.shape[1]
    pd = p.shape[2]
    row_spec = lambda w: pl.BlockSpec((tm, w), lambda i: (i, 0))
    out_shape = [jax.ShapeDtypeStruct((n, d), F32)]
    out_specs = [row_spec(d)]
    if want_v:
        out_shape.append(jax.ShapeDtypeStruct((n, aw), F32))
        out_specs.append(row_spec(aw))
    res = pl.pallas_call(
        functools.partial(_layer_a_kernel, chunk=chunk),
        grid=(n // tm,),
        in_specs=[row_spec(d), _layer_p_spec(p, layer, tm, lambda i: i), _const_spec((1, d)), _stacked_spec(win, mixer),
                  _const_spec((A_GROUPS, CHUNK, CHUNK)), _const_spec((CHUNK, aw)), _const_spec((1, aw)),
                  _stacked_spec(wout, mixer), _const_spec((1, d)), _stacked_spec(wg, layer), _stacked_spec(wpe, layer)],
        out_specs=out_specs,
        out_shape=out_shape,
        compiler_params=_params(("arbitrary",)),
        name="layer_a",
    )(x, p, gn, win, ws_t, bs_t, gv, wout, gpe, wg, wpe)
    return res if want_v else (res[0], None)


def _proj_b_kernel(x_ref, gn_ref, wqkv_ref, wiq_ref, wz_ref, wsmall_ref, wiwt_ref, gq_ref, gk_ref, gik_ref,
                   c128_ref, s128_ref, c64_ref, s64a_ref, s64b_ref,
                   q_ref, k_ref, v_ref, ik_ref, kbf_ref, vt_ref, ikbf_ref, iq_ref, iw_ref, iwt_ref, sz_ref, *, kc):
    dh = LANES
    bw = N_HEADS * dh
    kvw = N_KV_HEADS * dh
    iqw = IDX_HEADS * IDX_DIM
    x = x_ref[...]
    tm = x.shape[0]
    h = _bf(_rms(x, gn_ref[...]))
    small = _dot(h, wsmall_ref[...])
    iwt_ref[...] = _dot_nt(wiwt_ref[...], h)
    m_iq = _dot(h, wiq_ref[...])
    m_q = _dot(h, wqkv_ref[:, 0:bw])
    m_k = _dot(h, wqkv_ref[:, bw:bw + kvw])
    v = _dot(h, wqkv_ref[:, bw + kvw:bw + 2 * kvw])
    m_z = _dot(h, wz_ref[...])
    c128, s128 = c128_ref[...], s128_ref[...]
    c64, s64a, s64b = c64_ref[...], s64a_ref[...], s64b_ref[...]

    def rope128(t):
        return t * c128 + pltpu.roll(t, dh // 2, 1) * s128

    def rope64(t):
        return t * c64 + pltpu.roll(t, LANES - IDX_DIM // 2, 1) * s64a + pltpu.roll(t, IDX_DIM // 2, 1) * s64b

    lane = lax.broadcasted_iota(I32, (tm, LANES), 1)
    ms = jnp.sum(jnp.where(lane < IDX_DIM, small * small, 0.0), axis=-1, keepdims=True) * (1.0 / IDX_DIM)
    ikr = rope64(small * lax.rsqrt(ms + EPS) * gik_ref[...])[:, :IDX_DIM]
    ik_ref[...] = ikr
    ikbf_ref[...] = _bf(ikr)
    iw_ref[...] = small[:, IDX_DIM:IDX_DIM + IDX_HEADS]
    for j in range(iqw // LANES):
        iq_ref[:, j * LANES:(j + 1) * LANES] = _bf(rope64(m_iq[:, j * LANES:(j + 1) * LANES]))
    for hh in range(N_HEADS):
        qh = rope128(_rms(m_q[:, hh * dh:(hh + 1) * dh], gq_ref[...]))
        q_ref[:, hh * dh:(hh + 1) * dh] = _bf(qh * Q_SCALE)
    for hh in range(N_KV_HEADS):
        kh = rope128(_rms(m_k[:, hh * dh:(hh + 1) * dh], gk_ref[...]))
        k_ref[pl.ds(hh, tm, stride=N_KV_HEADS), :] = kh
        v_ref[pl.ds(hh, tm, stride=N_KV_HEADS), :] = v[:, hh * dh:(hh + 1) * dh]
        kbf_ref[:, hh * dh:(hh + 1) * dh] = _bf(kh)
    for j in range(tm // kc):
        vt_ref[j] = _bf(v[j * kc:(j + 1) * kc, :].T)
    sz_ref[...] = _silu(m_z)


def _proj_b(x, gn, win_b, wz, wsmall, wiwt, gq, gk, gik, tabs, *, mixer, tm, kc):
    n, d = x.shape
    bw = N_HEADS * LANES
    kvw = N_KV_HEADS * LANES
    iqw = IDX_HEADS * IDX_DIM
    qkvw = bw + 2 * kvw
    assert qkvw % iqw == 0
    period = tabs[0].shape[0] // tm
    row_spec = lambda w: pl.BlockSpec((tm, w), lambda i: (i, 0))
    tab_spec = pl.BlockSpec((tm, LANES), lambda i: (i % period, 0))
    sds = jax.ShapeDtypeStruct
    outs = [
        (sds((n, bw), BF16), row_spec(bw)),
        (sds((n * N_KV_HEADS, LANES), F32), pl.BlockSpec((tm * N_KV_HEADS, LANES), lambda i: (i, 0))),
        (sds((n * N_KV_HEADS, LANES), F32), pl.BlockSpec((tm * N_KV_HEADS, LANES), lambda i: (i, 0))),
        (sds((n, IDX_DIM), F32), row_spec(IDX_DIM)),
        (sds((n, kvw), BF16), row_spec(kvw)),
        (sds((n // kc, kvw, kc), BF16), pl.BlockSpec((tm // kc, kvw, kc), lambda i: (i, 0, 0))),
        (sds((n, IDX_DIM), BF16), row_spec(IDX_DIM)),
        (sds((n, iqw), BF16), row_spec(iqw)),
        (sds((n, IDX_HEADS), F32), row_spec(IDX_HEADS)),
        (sds((IDX_HEADS, n), F32), pl.BlockSpec((IDX_HEADS, tm), lambda i: (0, i))),
        (sds((n, bw), F32), row_spec(bw)),
    ]
    return pl.pallas_call(
        functools.partial(_proj_b_kernel, kc=kc),
        grid=(n // tm,),
        in_specs=[row_spec(d), _const_spec((1, d)),
                  pl.BlockSpec((None, d, qkvw), lambda i: (mixer, 0, 0), pipeline_mode=pl.Buffered(1)),
                  pl.BlockSpec((None, d, iqw), lambda i: (mixer, 0, qkvw // iqw), pipeline_mode=pl.Buffered(1)),
                  _const_spec(wz.shape), _const_spec(wsmall.shape),
                  _const_spec(wiwt.shape), _const_spec((1, LANES)), _const_spec((1, LANES)), _const_spec((1, LANES)),
                  tab_spec, tab_spec, tab_spec, tab_spec, tab_spec],
        out_specs=[o[1] for o in outs],
        out_shape=[o[0] for o in outs],
        compiler_params=_params(("arbitrary",)),
        name="proj_b",
    )(x, gn, win_b, win_b, wz, wsmall, wiwt, gq, gk, gik, *tabs)


def _dsa_prompt_sample_kernel(pt_ref, *refs, topk, sample):
    prompt_in, sample_in = refs[:13], refs[13:20]
    xo_ref, y_ref = refs[20:22]
    prompt_scratch, sample_scratch = refs[22:26], refs[26:30]
    step = pl.program_id(0) * pl.num_programs(1) + pl.program_id(1)
    nsteps = pl.num_programs(0) * pl.num_programs(1)
    _sample_attention_step(pt_ref, *sample_in, y_ref, *sample_scratch, step, nsteps, **sample)
    _dsa_prompt_kernel(*prompt_in, xo_ref, *prompt_scratch, topk=topk)


def _dsa_prompt_kernel(iq_ref, iwt_ref, ikbf_ref, kbf_ref, vt_ref, q_ref, sz_ref, x_ref, p_ref,
                       wout_ref, gpe_ref, wg_ref, wpe_ref, xo_ref, sc_ref, m_sc, l_sc, acc_sc, *, topk):
    tq = q_ref.shape[0]
    s_len = sc_ref.shape[0]
    dh = LANES
    i = pl.program_id(1)
    nch = i + 1
    qpos = i * tq + lax.broadcasted_iota(I32, (1, tq), 1)
    krow = lax.broadcasted_iota(I32, (tq, tq), 0)
    idx_scale = IDX_DIM ** -0.5 * IDX_HEADS ** -0.5

    def chunk_start(c):
        return pl.multiple_of(c * tq, tq)

    def idx_dots(c):
        ikc = ikbf_ref[pl.ds(chunk_start(c), tq), :]
        return [_dot_nt(ikc, iq_ref[:, h * IDX_DIM:(h + 1) * IDX_DIM]) for h in range(IDX_HEADS)]

    def idx_combine(c, xs):
        off = chunk_start(c)
        s = jnp.maximum(xs[0], 0.0) * iwt_ref[0:1, :]
        for h in range(1, IDX_HEADS):
            s = s + jnp.maximum(xs[h], 0.0) * iwt_ref[h:h + 1, :]
        sc_ref[pl.ds(off, tq), :] = jnp.where(off + krow <= qpos, s * idx_scale, -jnp.inf)

    def build_pair(j, carry):
        xs0 = idx_dots(2 * j)
        xs1 = idx_dots(2 * j + 1)
        idx_combine(2 * j, xs0)
        idx_combine(2 * j + 1, xs1)
        return carry

    lax.fori_loop(0, nch // 2, build_pair, 0)

    @pl.when(nch % 2 == 1)
    def _():
        idx_combine(nch - 1, idx_dots(nch - 1))

    def count(pred):
        def body(c, acc):
            off = chunk_start(c)
            m = jnp.where(pred(sc_ref[pl.ds(off, tq), :], off + krow), 1.0, 0.0)
            part = m[0:32]
            for r in range(1, tq // 32):
                part = part + m[r * 32:(r + 1) * 32]
            return acc + part
        acc = lax.fori_loop(0, nch, body, jnp.zeros((32, tq), F32))
        return jnp.sum(acc, axis=0, keepdims=True)

    kk = float(topk)
    t_thr, cnt_thr = _topk_threshold(lambda t: count(lambda blk, pos: blk >= t), (1, tq), kk)

    need = cnt_thr > kk

    @pl.when(jnp.max(jnp.where(need, 1.0, 0.0)) > 0.0)
    def _():
        quota = kk - count(lambda blk, pos: blk > t_thr)
        last = _tie_cutoff(lambda cand: count(lambda blk, pos: jnp.logical_and(blk == t_thr, pos < cand)),
                           (1, tq), quota, max(1, (s_len - 1).bit_length()))

        def demote(c, carry):
            off = chunk_start(c)
            blk = sc_ref[pl.ds(off, tq), :]
            drop = jnp.logical_and(jnp.logical_and(blk == t_thr, off + krow > last), need)
            sc_ref[pl.ds(off, tq), :] = jnp.where(drop, -jnp.inf, blk)
            return carry

        lax.fori_loop(0, nch, demote, 0)

    m_sc[...] = jnp.full(m_sc.shape, NEG, F32)
    l_sc[...] = jnp.zeros(l_sc.shape, F32)
    acc_sc[...] = jnp.zeros(acc_sc.shape, F32)
    rep = N_HEADS // N_KV_HEADS
    sub = SUBLANES

    def qk_dots(c):
        kc_all = kbf_ref[pl.ds(chunk_start(c), tq), :]
        return [_dot_nt(kc_all[:, (h // rep) * dh:(h // rep + 1) * dh], q_ref[:, h * dh:(h + 1) * dh])
                for h in range(N_HEADS)]

    def softmax_pv(c, qk):
        bias = jnp.where(sc_ref[pl.ds(chunk_start(c), tq), :] >= t_thr, 0.0, NEG)
        for g in range(N_KV_HEADS):
            vg = vt_ref[c, g * dh:(g + 1) * dh, :]
            for r in range(rep):
                h = g * rep + r
                st = slice(h * sub, (h + 1) * sub)
                s = (bias + qk[h]).reshape(tq // sub, sub, tq)
                m_old = m_sc[st, :]
                m_new = jnp.maximum(m_old, jnp.max(jnp.max(s, axis=0), axis=0, keepdims=True))
                p = jnp.exp2(s - m_new[None])
                alpha = jnp.exp2(m_old - m_new)
                l_sc[st, :] = alpha * l_sc[st, :] + jnp.sum(p, axis=0)
                acc_sc[h * dh:(h + 1) * dh, :] = (alpha[0:1] * acc_sc[h * dh:(h + 1) * dh, :]
                                                  + _dot(vg, _bf(p.reshape(tq, tq))))
                m_sc[st, :] = m_new

    def attend_pair(j, carry):
        qk0 = qk_dots(2 * j)
        qk1 = qk_dots(2 * j + 1)
        softmax_pv(2 * j, qk0)
        softmax_pv(2 * j + 1, qk1)
        return carry

    lax.fori_loop(0, nch // 2, attend_pair, 0)

    @pl.when(nch % 2 == 1)
    def _():
        softmax_pv(nch - 1, qk_dots(nch - 1))

    inv = [1.0 / jnp.sum(l_sc[h * sub:(h + 1) * sub, :], axis=0, keepdims=True) for h in range(N_HEADS)]
    att_t = jnp.concatenate([acc_sc[h * dh:(h + 1) * dh, :] * inv[h] for h in range(N_HEADS)], axis=0)
    y = _bf(att_t.T * sz_ref[...])
    x1 = x_ref[...] + _dot(y, wout_ref[...])
    xo_ref[...] = _per_layer_embed(x1, p_ref[...], gpe_ref[...], wg_ref[...], wpe_ref[...])


def _dsa_prompt(pr, x, p, wout, gpe, wg, wpe, *, layer, mixer, bsz, s_len, tq, topk, sample=None):
    q, _, _, _, kbf, vt, ikbf, iq, _, iwt, sz = pr
    n, d = x.shape
    bw = q.shape[1]
    kvw = kbf.shape[1]
    nq = s_len // tq
    blk = lambda w: pl.BlockSpec((tq, w), lambda b, i, *_: (b * nq + i, 0))
    per_b = lambda w: pl.BlockSpec((s_len, w), lambda b, i, *_: (b, 0))
    in_specs = [blk(iq.shape[1]),
                pl.BlockSpec((IDX_HEADS, tq), lambda b, i, *_: (0, b * nq + i)),
                per_b(IDX_DIM), per_b(kvw),
                pl.BlockSpec((nq, kvw, tq), lambda b, i, *_: (b, 0, 0)),
                blk(bw), blk(bw), blk(d), _layer_p_spec(p, layer, tq, lambda b, i, *_: b * nq + i),
                _stacked_spec(wout, mixer), _const_spec((1, d)), _stacked_spec(wg, layer), _stacked_spec(wpe, layer)]
    scratch = [pltpu.VMEM((s_len, tq), F32), pltpu.VMEM((N_HEADS * SUBLANES, tq), F32),
               pltpu.VMEM((N_HEADS * SUBLANES, tq), F32), pltpu.VMEM((bw, tq), F32)]
    args = (iq, iwt, ikbf, kbf, vt, q, sz, x, p, wout, gpe, wg, wpe)
    if sample is None:
        return pl.pallas_call(
            functools.partial(_dsa_prompt_kernel, topk=topk),
            grid=(bsz, nq), in_specs=in_specs, out_specs=blk(d),
            out_shape=jax.ShapeDtypeStruct((n, d), F32), scratch_shapes=scratch,
            compiler_params=_params(("arbitrary", "arbitrary")), name="dsa_prompt",
        )(*args)

    page_table, q_s, bias, k_new, v_new, sz_s, cache_k, cache_v, cache_layer, group, t_new = sample
    n_pages = page_table.shape[1]
    lp = n_pages * PAGE_SIZE + LANES
    rows = group * t_new
    sblk = lambda r, w: pl.BlockSpec((r, w), lambda b, i, *_: (b * nq + i, 0))
    any_spec = pl.BlockSpec(memory_space=pl.ANY)
    return pl.pallas_call(
        functools.partial(_dsa_prompt_sample_kernel, topk=topk,
                          sample=dict(group=group, n_pages=n_pages, layer=cache_layer, t_new=t_new)),
        grid_spec=pltpu.PrefetchScalarGridSpec(
            num_scalar_prefetch=1,
            grid=(bsz, nq),
            in_specs=in_specs + [sblk(rows, bw), sblk(rows, lp), sblk(rows * N_KV_HEADS, LANES),
                                 sblk(rows * N_KV_HEADS, LANES), sblk(rows, bw), any_spec, any_spec],
            out_specs=[blk(d), sblk(rows, bw)],
            scratch_shapes=scratch + [pltpu.VMEM((2, group, lp * N_KV_HEADS, LANES), F32),
                                      pltpu.VMEM((2, group, lp * N_KV_HEADS, LANES), F32),
                                      pltpu.SemaphoreType.DMA((2,)), pltpu.SemaphoreType.DMA((2,))]),
        out_shape=[jax.ShapeDtypeStruct((n, d), F32), jax.ShapeDtypeStruct(q_s.shape, F32)],
        compiler_params=_params(("arbitrary", "arbitrary")),
        name="dsa_prompt_sample",
    )(page_table, *args, q_s, bias, k_new, v_new, sz_s, cache_k, cache_v)


def _page_copies(pt_ref, cache_ref, buf, sem, step, slot, *, group, n_pages, layer, rows_per_page, axis):
    copies = []
    for b in range(group):
        for n in range(n_pages):
            page = pt_ref[step * group + b, n]
            span = pl.ds(n * rows_per_page, rows_per_page)
            dst = buf.at[slot, b, span, :] if axis == 0 else buf.at[slot, b, :, span]
            copies.append(pltpu.make_async_copy(cache_ref.at[page, layer], dst, sem.at[slot]))
    return copies


def _paged_pipeline(make_copies_list, step, nsteps):
    slot = step % 2

    def start_all(at_step, at_slot):
        for mk in make_copies_list:
            for n, cp in enumerate(mk(at_step, at_slot)):
                cp.start(priority=n % 2)

    @pl.when(step == 0)
    def _():
        start_all(step, slot)

    @pl.when(step + 1 < nsteps)
    def _():
        start_all(step + 1, 1 - slot)

    for mk in make_copies_list:
        for cp in mk(step, slot):
            cp.wait()
    return slot


def _idx_sample_kernel(pt_ref, iq_ref, iw_ref, iknew_ref, cik_ref, bias_ref, ikbuf, sem, sc_ref,
                       *, group, n_pages, layer, t_new, topk):
    past = n_pages * PAGE_SIZE
    lp = past + LANES
    slot = _paged_pipeline([functools.partial(_page_copies, pt_ref, cik_ref, ikbuf, sem, group=group,
                                              n_pages=n_pages, layer=layer, rows_per_page=PAGE_SIZE, axis=1)],
                           pl.program_id(0), pl.num_programs(0))
    idx_scale = IDX_DIM ** -0.5 * IDX_HEADS ** -0.5
    iq_all = iq_ref[...].astype(F32)
    iknew_all = iknew_ref[...].astype(F32)
    iw_all = iw_ref[...]
    kpos = lax.broadcasted_iota(I32, (t_new, lp), 1)
    qpos = past + lax.broadcasted_iota(I32, (t_new, lp), 0)
    dots = []
    for b in range(group):
        rows = slice(b * t_new, (b + 1) * t_new)
        iqb = iq_all[rows]
        iqs = _bf(jnp.concatenate([iqb[:, h * IDX_DIM:(h + 1) * IDX_DIM] for h in range(IDX_HEADS)], axis=0))
        ik_new = _bf(jnp.concatenate([iknew_all[rows], jnp.zeros((LANES - t_new, IDX_DIM), F32)], axis=0))
        dots.append((_dot(iqs, _bf(ikbuf[slot, b])), _dot_nt(iqs, ik_new)))
    for b in range(group):
        rows = slice(b * t_new, (b + 1) * t_new)
        xp, xn = dots[b]
        iwb = iw_all[rows]
        sp = jnp.zeros((t_new, past), F32)
        sn = jnp.zeros((t_new, LANES), F32)
        for h in range(IDX_HEADS):
            w = iwb[:, h:h + 1]
            sp = sp + jnp.maximum(xp[h * t_new:(h + 1) * t_new], 0.0) * w
            sn = sn + jnp.maximum(xn[h * t_new:(h + 1) * t_new], 0.0) * w
        sc = jnp.concatenate([sp, sn], axis=1) * idx_scale
        sc_ref[rows, :] = jnp.where(kpos <= qpos, sc, -jnp.inf)

    kk = float(topk)
    rows_all = group * t_new
    kpos_all = lax.broadcasted_iota(I32, (rows_all, lp), 1)

    def count(pred):
        return jnp.sum(jnp.where(pred(sc_ref[...]), 1.0, 0.0), axis=1, keepdims=True)

    t_thr, cnt_thr = _topk_threshold(lambda t: count(lambda blk: blk >= t), (rows_all, 1), kk)
    need = cnt_thr > kk

    @pl.when(jnp.max(jnp.where(need, 1.0, 0.0)) > 0.0)
    def _():
        quota = kk - count(lambda blk: blk > t_thr)
        last = _tie_cutoff(lambda cand: count(lambda blk: jnp.logical_and(blk == t_thr, kpos_all < cand)),
                           (rows_all, 1), quota, max(1, (lp - 1).bit_length()))
        blk = sc_ref[...]
        drop = jnp.logical_and(jnp.logical_and(blk == t_thr, kpos_all > last), need)
        sc_ref[...] = jnp.where(drop, -jnp.inf, blk)

    bias_ref[...] = jnp.where(sc_ref[...] >= t_thr, 0.0, NEG)


def _idx_sample(page_table, iq, iw, ikbf, cache_ik, *, layer, group, t_new, topk):
    n = iq.shape[0]
    db, n_pages = page_table.shape
    past = n_pages * PAGE_SIZE
    lp = past + LANES
    rows = group * t_new
    blk = lambda w: pl.BlockSpec((rows, w), lambda i, pt: (i, 0))
    return pl.pallas_call(
        functools.partial(_idx_sample_kernel, group=group, n_pages=n_pages, layer=layer, t_new=t_new, topk=topk),
        grid_spec=pltpu.PrefetchScalarGridSpec(
            num_scalar_prefetch=1,
            grid=(db // group,),
            in_specs=[blk(iq.shape[1]), blk(IDX_HEADS), blk(IDX_DIM), pl.BlockSpec(memory_space=pl.ANY)],
            out_specs=blk(lp),
            scratch_shapes=[pltpu.VMEM((2, group, IDX_DIM, past), F32), pltpu.SemaphoreType.DMA((2,)),
                            pltpu.VMEM((rows, lp), F32)]),
        out_shape=jax.ShapeDtypeStruct((n, lp), F32),
        compiler_params=_params(("arbitrary",)),
        name="idx_sample",
    )(page_table, iq, iw, ikbf, cache_ik)


def _attn_sample_kernel(pt_ref, q_ref, bias_ref, knew_ref, vnew_ref, sz_ref, ck_ref, cv_ref, y_ref,
                        kbuf, vbuf, ksem, vsem, **static):
    _sample_attention_step(pt_ref, q_ref, bias_ref, knew_ref, vnew_ref, sz_ref, ck_ref, cv_ref, y_ref,
                           kbuf, vbuf, ksem, vsem, pl.program_id(0), pl.num_programs(0), **static)


def _sample_attention_step(pt_ref, q_ref, bias_ref, knew_ref, vnew_ref, sz_ref, ck_ref, cv_ref, y_ref,
                           kbuf, vbuf, ksem, vsem, step, nsteps, *, group, n_pages, layer, t_new):
    past = n_pages * PAGE_SIZE
    lp = past + LANES
    dh = LANES
    nkv = N_KV_HEADS
    rep = N_HEADS // nkv
    copy_args = dict(group=group, n_pages=n_pages, layer=layer, rows_per_page=PAGE_SIZE * nkv, axis=0)
    slot = _paged_pipeline([functools.partial(_page_copies, pt_ref, ck_ref, kbuf, ksem, **copy_args),
                            functools.partial(_page_copies, pt_ref, cv_ref, vbuf, vsem, **copy_args)],
                           step, nsteps)
    q_all = q_ref[...].astype(F32)
    pad = jnp.zeros(((LANES - t_new) * nkv, dh), F32)
    qk = {}
    for b in range(group):
        rows = slice(b * t_new, (b + 1) * t_new)
        new_rows = slice(b * t_new * nkv, (b + 1) * t_new * nkv)
        tail = pl.ds(past * nkv, LANES * nkv)
        kbuf[slot, b, tail, :] = jnp.concatenate([knew_ref[new_rows, :], pad], axis=0)
        vbuf[slot, b, tail, :] = jnp.concatenate([vnew_ref[new_rows, :], pad], axis=0)
        qb = q_all[rows]
        for g in range(nkv):
            kg = _bf(kbuf[slot, b, pl.ds(g, lp, stride=nkv), :])
            qg = _bf(jnp.concatenate([qb[:, (g * rep + r) * dh:(g * rep + r + 1) * dh] for r in range(rep)], axis=0))
            qk[b, g] = _dot_nt(qg, kg)
    for b in range(group):
        rows = slice(b * t_new, (b + 1) * t_new)
        bias = jnp.concatenate([bias_ref[rows, :]] * rep, axis=0)
        outs = []
        for g in range(nkv):
            vg = _bf(vbuf[slot, b, pl.ds(g, lp, stride=nkv), :])
            s = bias + qk[b, g]
            m = jnp.max(s, axis=1, keepdims=True)
            p = jnp.exp2(s - m)
            l = jnp.sum(p, axis=1, keepdims=True)
            o = _dot(_bf(p), vg) * (1.0 / l)
            outs.extend(o[r * t_new:(r + 1) * t_new] for r in range(rep))
        y_ref[rows, :] = jnp.concatenate(outs, axis=1) * sz_ref[rows, :]


def _attn_sample(page_table, q, bias, knew, vnew, sz, cache_k, cache_v, *, layer, group, t_new):
    n, bw = q.shape
    db, n_pages = page_table.shape
    past = n_pages * PAGE_SIZE
    lp = past + LANES
    rows = group * t_new
    blk = lambda w: pl.BlockSpec((rows, w), lambda i, pt: (i, 0))
    new_blk = pl.BlockSpec((rows * N_KV_HEADS, LANES), lambda i, pt: (i, 0))
    return pl.pallas_call(
        functools.partial(_attn_sample_kernel, group=group, n_pages=n_pages, layer=layer, t_new=t_new),
        grid_spec=pltpu.PrefetchScalarGridSpec(
            num_scalar_prefetch=1,
            grid=(db // group,),
            in_specs=[blk(bw), blk(lp), new_blk, new_blk, blk(bw),
                      pl.BlockSpec(memory_space=pl.ANY), pl.BlockSpec(memory_space=pl.ANY)],
            out_specs=blk(bw),
            scratch_shapes=[pltpu.VMEM((2, group, lp * N_KV_HEADS, LANES), F32),
                            pltpu.VMEM((2, group, lp * N_KV_HEADS, LANES), F32),
                            pltpu.SemaphoreType.DMA((2,)), pltpu.SemaphoreType.DMA((2,))]),
        out_shape=jax.ShapeDtypeStruct((n, bw), F32),
        compiler_params=_params(("arbitrary",)),
        name="attn_sample",
    )(page_table, q, bias, knew, vnew, sz, cache_k, cache_v)


def _out_ple_kernel(y_ref, x_ref, p_ref, wout_ref, gpe_ref, wg_ref, wpe_ref, xo_ref):
    x1 = x_ref[...] + _dot(_bf(y_ref[...]), wout_ref[...])
    xo_ref[...] = _per_layer_embed(x1, p_ref[...], gpe_ref[...], wg_ref[...], wpe_ref[...])


def _out_ple(y, x, p, wout, gpe, wg, wpe, *, layer, mixer, tm):
    n, d = x.shape
    row_spec = lambda w: pl.BlockSpec((tm, w), lambda i: (i, 0))
    return pl.pallas_call(
        _out_ple_kernel,
        grid=(n // tm,),
        in_specs=[row_spec(y.shape[1]), row_spec(d), _layer_p_spec(p, layer, tm, lambda i: i),
                  _stacked_spec(wout, mixer), _const_spec((1, d)), _stacked_spec(wg, layer), _stacked_spec(wpe, layer)],
        out_specs=row_spec(d),
        out_shape=jax.ShapeDtypeStruct((n, d), F32),
        compiler_params=_params(("arbitrary",)),
        name="out_ple",
    )(y, x, p, wout, gpe, wg, wpe)


def _rope_tables(pos):
    def angles(half):
        inv = ROPE_THETA ** (-jnp.arange(half, dtype=F32) / half)
        return pos.astype(F32)[:, None] * inv[None, :]
    a = angles(LANES // 2)
    c128 = jnp.concatenate([jnp.cos(a), jnp.cos(a)], axis=1)
    s128 = jnp.concatenate([-jnp.sin(a), jnp.sin(a)], axis=1)
    a = angles(IDX_DIM // 2)
    zero = jnp.zeros_like(a)
    c64 = jnp.tile(jnp.cos(a), (1, 4))
    s64a = jnp.tile(jnp.concatenate([-jnp.sin(a), zero], axis=1), (1, 2))
    s64b = jnp.tile(jnp.concatenate([zero, jnp.sin(a)], axis=1), (1, 2))
    return c128, s128, c64, s64a, s64b


def _tile_rows(n, pref):
    t = min(n, pref)
    assert n % t == 0, (n, t)
    return t


def kernel(x_prompt, x_sample, cache_k, cache_v, cache_ik, page_table, p_prompt, p_sample, g_norm, w_in_a, w_s_a, b_s_a, g_v_a, w_out_a, w_in_b, g_q_b, g_k_b, g_ik_b, w_out_b, g_pe, w_pe_gate, w_pe_proj):
    bsz, s_len, d = x_prompt.shape
    db, t_new, _ = x_sample.shape
    depth = g_norm.shape[0]
    n_pool, nb_layers = cache_k.shape[0], cache_k.shape[1]
    n_pages = page_table.shape[1]
    past = n_pages * PAGE_SIZE
    aw = g_v_a.shape[1]
    bw = N_HEADS * LANES
    kvw = N_KV_HEADS * LANES
    iqw = IDX_HEADS * IDX_DIM
    assert d == bw and s_len % CHUNK == 0 and CHUNK % t_new == 0 and t_new % 8 == 0
    assert w_in_b.shape[2] == 2 * bw + 2 * kvw + iqw + IDX_DIM + IDX_HEADS

    n_p, n_s = bsz * s_len, db * t_new
    yp = x_prompt.reshape(n_p, d)
    ys = x_sample.reshape(n_s, d)
    pp = p_prompt.reshape(depth, n_p, -1)
    ps = p_sample.reshape(depth, n_s, -1)
    ck = cache_k.reshape(n_pool, nb_layers, PAGE_SIZE * N_KV_HEADS, LANES)
    cv = cache_v.reshape(n_pool, nb_layers, PAGE_SIZE * N_KV_HEADS, LANES)
    cik = jnp.swapaxes(cache_ik, 2, 3)

    tm_p = _tile_rows(n_p, 256)
    tm_s = _tile_rows(n_s, 256)
    tq = _tile_rows(s_len, 256)
    topk_p = min(TOPK_MAX, s_len // 4)
    topk_s = min(TOPK_MAX, (past + t_new) // 4)
    idx_group = _tile_rows(db, 32)
    att_group = _tile_rows(db, 2)

    tabs_p = _rope_tables(jnp.arange(s_len))
    tabs_s = _rope_tables(past + (jnp.arange(tm_s) % t_new))

    gn = g_norm.reshape(depth, 1, d)
    gpe = g_pe.reshape(depth, 1, d)
    wg = _bf(w_pe_gate)
    wpe = _bf(w_pe_proj)
    win_a = _bf(w_in_a)
    wout_a = _bf(w_out_a)
    wout_b = _bf(w_out_b)
    win_b = _bf(w_in_b)

    kp_l, vp_l, ikp_l, ks_l, vs_l, iks_l, cv_l = [], [], [], [], [], [], []
    for i in range(depth):
        j = i // 2
        if i % 2 == 0:
            gv = g_v_a[j].reshape(1, aw)
            reps = CHUNK // t_new
            ws_p = w_s_a[j]
            bs_p = jnp.repeat(b_s_a[j].T, aw // A_GROUPS, axis=1)
            ws_s = jnp.tile(w_s_a[j][:, :t_new, :t_new], (1, reps, reps))
            bs_s = jnp.tile(jnp.repeat(b_s_a[j][:, :t_new].T, aw // A_GROUPS, axis=1), (reps, 1))
            yp, _ = _layer_a(yp, pp, gn[i], win_a, ws_p, bs_p, gv, wout_a, gpe[i], wg, wpe,
                             layer=i, mixer=j, chunk=CHUNK, tm=tm_p, want_v=False)
            ys, v_rows = _layer_a(ys, ps, gn[i], win_a, ws_s, bs_s, gv, wout_a, gpe[i], wg, wpe,
                                  layer=i, mixer=j, chunk=t_new, tm=tm_s, want_v=True)
            cv_l.append(v_rows.reshape(db, t_new, aw))
        else:
            w = w_in_b[j]
            o_iq = bw + 2 * kvw
            o_ik = o_iq + iqw
            o_iw = o_ik + IDX_DIM
            o_z = o_iw + IDX_HEADS
            wz = _bf(w[:, o_z:])
            wsmall = _bf(jnp.pad(w[:, o_ik:o_z], ((0, 0), (0, LANES - IDX_DIM - IDX_HEADS))))
            wiwt = _bf(w[:, o_iw:o_z].T)
            gq = g_q_b[j].reshape(1, LANES)
            gk = g_k_b[j].reshape(1, LANES)
            gik = jnp.pad(g_ik_b[j], (0, LANES - IDX_DIM)).reshape(1, LANES)

            pr = _proj_b(yp, gn[i], win_b, wz, wsmall, wiwt, gq, gk, gik, tabs_p, mixer=j, tm=tm_p, kc=tq)
            kp_l.append(pr[1].reshape(bsz, s_len, N_KV_HEADS, LANES))
            vp_l.append(pr[2].reshape(bsz, s_len, N_KV_HEADS, LANES))
            ikp_l.append(pr[3].reshape(bsz, s_len, IDX_DIM))
            sr = _proj_b(ys, gn[i], win_b, wz, wsmall, wiwt, gq, gk, gik, tabs_s, mixer=j, tm=tm_s,
                         kc=min(tm_s, LANES))
            q_s, k_s, v_s, ik_s, _, _, ikbf_s, iq_s, iw_s, _, sz_s = sr
            ks_l.append(k_s.reshape(db, t_new, N_KV_HEADS, LANES))
            vs_l.append(v_s.reshape(db, t_new, N_KV_HEADS, LANES))
            iks_l.append(ik_s.reshape(db, t_new, IDX_DIM))
            bias = _idx_sample(page_table, iq_s, iw_s, ikbf_s, cik, layer=j, group=idx_group, t_new=t_new, topk=topk_s)

            dsa_args = dict(layer=i, mixer=j, bsz=bsz, s_len=s_len, tq=tq, topk=topk_p)
            prompt_steps = bsz * (s_len // tq)
            if db % prompt_steps == 0:
                yp, y_s = _dsa_prompt(pr, yp, pp, wout_b, gpe[i], wg, wpe, **dsa_args,
                                      sample=(page_table, q_s, bias, k_s, v_s, sz_s, ck, cv, j,
                                              db // prompt_steps, t_new))
            else:
                yp = _dsa_prompt(pr, yp, pp, wout_b, gpe[i], wg, wpe, **dsa_args)
                y_s = _attn_sample(page_table, q_s, bias, k_s, v_s, sz_s, ck, cv, layer=j, group=att_group, t_new=t_new)
            ys = _out_ple(y_s, ys, ps, wout_b, gpe[i], wg, wpe, layer=i, mixer=j, tm=tm_s)

    return (yp.reshape(bsz, s_len, d), ys.reshape(db, t_new, d),
            jnp.stack(kp_l, axis=1), jnp.stack(vp_l, axis=1), jnp.stack(ikp_l, axis=1),
            jnp.stack(ks_l, axis=1), jnp.stack(vs_l, axis=1), jnp.stack(iks_l, axis=1),
            jnp.stack(cv_l, axis=1))
```
